```python
import math
import jax, jax.numpy as jnp
from jax import lax
import numpy as np

D_MODEL = 4096
BATCH = 1
SEQ = 8192
DEPTH = 2

CHUNK = 64
ML_HEADS = 8
ML_QK = 128
ML_V = 256
ML_WIDTH = ML_HEADS * ML_V
CV_WIDTH = 1024
CV_K = 3
DA_HEADS = 8
DA_QK = 64
DA_V = 2 * DA_QK
DA_WIDTH = DA_HEADS * DA_V
Q_BLOCK = 128
MIX_WIDTH = ML_WIDTH + CV_WIDTH + DA_WIDTH
N_BRANCH = 3
MEM_TOKENS = 256
XA_HEADS = 4
XA_DIM = 128
XA_WIDTH = XA_HEADS * XA_DIM
PEER_HEADS = 8
N_KEYS = 128
N_EXPERTS = N_KEYS * N_KEYS
PEER_QK = 256
PEER_HALF = PEER_QK // 2
PEER_TOPK = 16
PEER_BLOCK = 64
EPS = 1e-6

IN_SPLITS = (
    ML_HEADS * ML_QK, ML_HEADS * ML_QK, ML_WIDTH, ML_WIDTH, ML_HEADS, ML_HEADS,
    CV_WIDTH, CV_WIDTH, CV_WIDTH,
    DA_HEADS * 2 * DA_QK, DA_HEADS * 2 * DA_QK, DA_WIDTH,
)
IN_COLS = sum(IN_SPLITS)

kernel_name = "hybrid_mlstm_conv_diffattn_peer"


def rmsnorm(x, g):
    xf = x.astype(jnp.float32)
    y = xf * lax.rsqrt(jnp.mean(xf * xf, axis=-1, keepdims=True) + EPS)
    return (y * g.astype(jnp.float32)).astype(x.dtype)


def mlstm(q, k, v, i_pre, f_pre):
    B, S, H, dk = q.shape
    dv = v.shape[-1]
    nc = S // CHUNK
    f32 = jnp.float32

    def to_chunks(t):
        t = t.astype(f32).reshape((B, nc, CHUNK, H) + t.shape[3:])
        return jnp.swapaxes(jnp.moveaxis(t, 1, 0), 2, 3)

    qc = to_chunks(q) * (dk ** -0.5)
    kc = to_chunks(k)
    vc = to_chunks(v)
    ic = to_chunks(i_pre)
    lfc = jax.nn.log_sigmoid(to_chunks(f_pre))
    tril = jnp.tril(jnp.ones((CHUNK, CHUNK), dtype=bool))

    def step(carry, inp):
        C, n, m = carry
        qb, kb, vb, ib, lf = inp
        b = jnp.cumsum(lf, axis=-1)
        dmat = jnp.where(tril, b[..., :, None] - b[..., None, :] + ib[..., None, :], -jnp.inf)
        inter = b + m[..., None]
        mt = jnp.maximum(inter, jnp.max(dmat, axis=-1))
        w = jnp.exp(dmat - mt[..., None]) * jnp.einsum('bhtd,bhsd->bhts', qb, kb)
        s_prev = jnp.exp(inter - mt)
        num = jnp.einsum('bhts,bhsv->bhtv', w, vb) + s_prev[..., None] * jnp.einsum('bhtd,bhvd->bhtv', qb, C)
        den = jnp.sum(w, axis=-1) + s_prev * jnp.einsum('bhtd,bhd->bht', qb, n)
        h = num / jnp.maximum(jnp.abs(den), jnp.exp(-mt))[..., None]
        bl = b[..., -1]
        a = bl[..., None] - b + ib
        m_new = jnp.maximum(bl + m, jnp.max(a, axis=-1))
        decay = jnp.exp(bl + m - m_new)
        wa = jnp.exp(a - m_new[..., None])
        C_new = decay[..., None, None] * C + jnp.einsum('bhs,bhsv,bhsd->bhvd', wa, vb, kb)
        n_new = decay[..., None] * n + jnp.einsum('bhs,bhsd->bhd', wa, kb)
        return (C_new, n_new, m_new), h

    init = (jnp.zeros((B, H, dv, dk), f32), jnp.zeros((B, H, dk), f32), jnp.zeros((B, H), f32))
    _, hs = lax.scan(step, init, (qc, kc, vc, ic, lfc))
    return jnp.moveaxis(jnp.swapaxes(hs, 2, 3), 0, 1).reshape(B, S, H, dv)


def short_conv(bg, cg, hv, w):
    u = cg * hv
    y = lax.conv_general_dilated(u, w[:, None, :].astype(u.dtype), window_strides=(1,),
                                 padding=[(CV_K - 1, 0)], dimension_numbers=('NWC', 'WIO', 'NWC'),
                                 feature_group_count=u.shape[-1])
    return bg * y


def diff_attention(q, k, v, q_g, k_g, lam_p, subln_g, lam_init):
    B, S, H, _, dh = q.shape
    dv = v.shape[-1]
    nb = S // Q_BLOCK
    f32 = jnp.float32
    lp = lam_p.astype(f32)
    lam = jnp.exp(jnp.sum(lp[0] * lp[1])) - jnp.exp(jnp.sum(lp[2] * lp[3])) + lam_init
    qT = jnp.transpose(rmsnorm(q, q_g).astype(f32), (0, 2, 3, 1, 4)) * (dh ** -0.5)
    kT = jnp.transpose(rmsnorm(k, k_g).astype(f32), (0, 2, 3, 1, 4))
    vT = jnp.transpose(v.astype(f32), (0, 2, 1, 3))
    q_blocks = jnp.moveaxis(qT.reshape(B, H, 2, nb, Q_BLOCK, dh), 3, 0)
    chunk_id = jnp.arange(S) // CHUNK
    q_chunk = chunk_id.reshape(nb, Q_BLOCK)

    def block(args):
        qb, qc = args
        s = jnp.einsum('bhcqd,bhckd->bhcqk', qb, kT)
        mask = chunk_id[None, :] <= qc[:, None]
        p = jax.nn.softmax(jnp.where(mask, s, -jnp.inf), axis=-1)
        a = p[:, :, 0] - lam * p[:, :, 1]
        return jnp.einsum('bhqk,bhkv->bhqv', a, vT)

    o = lax.map(block, (q_blocks, q_chunk))
    o = jnp.transpose(jnp.moveaxis(o, 0, 2).reshape(B, H, S, dv), (0, 2, 1, 3))
    o = rmsnorm(o, subln_g) * (1.0 - lam_init)
    return o.reshape(B, S, H * dv).astype(v.dtype)


def memory_cross_attention(h, mem, mem_g, wq, wkv, q_g, k_g, wo):
    B, S, _ = h.shape
    M = mem.shape[1]
    f32 = jnp.float32
    q = rmsnorm((h @ wq).reshape(B, S, XA_HEADS, XA_DIM), q_g)
    kv = (rmsnorm(mem, mem_g) @ wkv).reshape(B, M, 2, XA_HEADS, XA_DIM)
    k = rmsnorm(kv[:, :, 0], k_g)
    v = kv[:, :, 1]
    s = jnp.einsum('bshd,bmhd->bhsm', q.astype(f32), k.astype(f32)) * (XA_DIM ** -0.5)
    p = jax.nn.softmax(s, axis=-1)
    o = jnp.einsum('bhsm,bmhd->bshd', p, v.astype(f32)).reshape(B, S, XA_WIDTH).astype(h.dtype)
    return o @ wo


def peer(h, wq, keys, u, v):
    B, S, D = h.shape
    nb = (B * S) // PEER_BLOCK
    f32 = jnp.float32
    xb = h.reshape(nb, PEER_BLOCK, D)
    k1 = keys[:, 0].astype(f32)
    k2 = keys[:, 1].astype(f32)

    def block(xt):
        q = (xt @ wq).astype(f32).reshape(PEER_BLOCK, PEER_HEADS, 2, PEER_HALF)
        s1 = jnp.einsum('thd,hnd->thn', q[:, :, 0], k1)
        s2 = jnp.einsum('thd,hnd->thn', q[:, :, 1], k2)
        t1, i1 = lax.top_k(s1, PEER_TOPK)
        t2, i2 = lax.top_k(s2, PEER_TOPK)
        cand = (t1[..., :, None] + t2[..., None, :]).reshape(PEER_BLOCK, PEER_HEADS, PEER_TOPK * PEER_TOPK)
        cidx = (i1[..., :, None] * N_KEYS + i2[..., None, :]).reshape(PEER_BLOCK, PEER_HEADS, PEER_TOPK * PEER_TOPK)
        top, pos = lax.top_k(cand, PEER_TOPK)
        eidx = jnp.take_along_axis(cidx, pos, axis=-1)
        g = jax.nn.softmax(top, axis=-1)
        act = jax.nn.gelu(jnp.einsum('td,thkd->thk', xt.astype(f32), u[eidx].astype(f32)))
        return jnp.einsum('thk,thkd->td', g * act, v[eidx].astype(f32)).astype(xt.dtype)

    return lax.map(block, xb).reshape(B, S, D)


def setup_inputs(seed: int = 0) -> dict:
    key = jax.random.key(seed)
    ks = iter(jax.random.split(key, 40))
    f32 = jnp.float32

    def nrm(shape, scale):
        return jax.random.normal(next(ks), shape, f32) * scale

    def gain(shape):
        return 1.0 + 0.02 * jax.random.normal(next(ks), shape, f32)

    D = D_MODEL
    inp = {}
    inp['x'] = nrm((BATCH, SEQ, D), 1.0)
    inp['mem'] = nrm((BATCH, MEM_TOKENS, D), 1.0)
    inp['mix_norm'] = gain((DEPTH, D))
    inp['w_in'] = nrm((DEPTH, D, IN_COLS), D ** -0.5)
    inp['b_if'] = jnp.stack([0.1 * jax.random.normal(next(ks), (DEPTH, ML_HEADS), f32),
                             3.0 + 0.5 * jax.random.normal(next(ks), (DEPTH, ML_HEADS), f32)], axis=1)
    inp['ml_norm'] = gain((DEPTH, ML_WIDTH))
    inp['conv_w'] = nrm((DEPTH, CV_K, CV_WIDTH), CV_K ** -0.5)
    inp['da_q_norm'] = gain((DEPTH, DA_QK))
    inp['da_k_norm'] = gain((DEPTH, DA_QK))
    inp['da_lambda'] = nrm((DEPTH, 4, DA_QK), 0.1)
    inp['da_subln'] = gain((DEPTH, DA_V))
    inp['w_gate'] = nrm((DEPTH, D, N_BRANCH * D), D ** -0.5)
    inp['b_gate'] = nrm((DEPTH, N_BRANCH * D), 0.02)
    inp['w_branch'] = jnp.concatenate([nrm((DEPTH, ML_WIDTH, D), ML_WIDTH ** -0.5),
                                       nrm((DEPTH, CV_WIDTH, D), CV_WIDTH ** -0.5),
                                       nrm((DEPTH, DA_WIDTH, D), DA_WIDTH ** -0.5)], axis=1)
    inp['w_out'] = nrm((DEPTH, D, D), D ** -0.5)
    inp['xa_norm'] = gain((DEPTH, D))
    inp['mem_norm'] = gain((DEPTH, D))
    inp['xa_wq'] = nrm((DEPTH, D, XA_WIDTH), D ** -0.5)
    inp['xa_wkv'] = nrm((DEPTH, D, 2 * XA_WIDTH), D ** -0.5)
    inp['xa_q_norm'] = gain((DEPTH, XA_DIM))
    inp['xa_k_norm'] = gain((DEPTH, XA_DIM))
    inp['xa_wo'] = nrm((DEPTH, XA_WIDTH, D), XA_WIDTH ** -0.5)
    inp['ffn_norm'] = gain((DEPTH, D))
    inp['peer_wq'] = nrm((DEPTH, D, PEER_HEADS * PEER_QK), D ** -0.5)
    inp['peer_keys'] = nrm((DEPTH, PEER_HEADS, 2, N_KEYS, PEER_HALF), PEER_HALF ** -0.5)
    inp['peer_u'] = nrm((DEPTH, N_EXPERTS, D), D ** -0.5)
    inp['peer_v'] = nrm((DEPTH, N_EXPERTS, D), PEER_HEADS ** -0.5)
    return inp


def reference(x, mem, mix_norm, w_in, b_if, ml_norm, conv_w, da_q_norm, da_k_norm, da_lambda, da_subln,
              w_gate, b_gate, w_branch, w_out, xa_norm, mem_norm, xa_wq, xa_wkv, xa_q_norm, xa_k_norm, xa_wo,
              ffn_norm, peer_wq, peer_keys, peer_u, peer_v):
    B, S, _ = x.shape
    offs = [int(o) for o in np.cumsum(IN_SPLITS)[:-1]]
    for l in range(DEPTH):
        h = rmsnorm(x, mix_norm[l])
        (qm, km, vm, om, ig, fg, cb, cc, ch, qd, kd, vd) = jnp.split(h @ w_in[l], offs, axis=-1)
        hm = mlstm(qm.reshape(B, S, ML_HEADS, ML_QK), km.reshape(B, S, ML_HEADS, ML_QK),
                   vm.reshape(B, S, ML_HEADS, ML_V), ig + b_if[l, 0], fg + b_if[l, 1])
        hm = (rmsnorm(hm, ml_norm[l].reshape(ML_HEADS, ML_V)).reshape(B, S, ML_WIDTH)
              * jax.nn.sigmoid(om.astype(jnp.float32))).astype(x.dtype)
        hc = short_conv(cb, cc, ch, conv_w[l])
        lam_init = 0.8 - 0.6 * math.exp(-0.3 * l)
        hd = diff_attention(qd.reshape(B, S, DA_HEADS, 2, DA_QK), kd.reshape(B, S, DA_HEADS, 2, DA_QK),
                            vd.reshape(B, S, DA_HEADS, DA_V), da_q_norm[l], da_k_norm[l], da_lambda[l],
                            da_subln[l], lam_init)
        gates = jax.nn.sigmoid(h @ w_gate[l] + b_gate[l]).reshape(B, S, N_BRANCH, D_MODEL)
        wb = w_branch[l]
        merged = (gates[:, :, 0] * (hm @ wb[:ML_WIDTH])
                  + gates[:, :, 1] * (hc @ wb[ML_WIDTH:ML_WIDTH + CV_WIDTH])
                  + gates[:, :, 2] * (hd @ wb[ML_WIDTH + CV_WIDTH:]))
        x = x + merged @ w_out[l]
        x = x + memory_cross_attention(rmsnorm(x, xa_norm[l]), mem, mem_norm[l], xa_wq[l], xa_wkv[l],
                                       xa_q_norm[l], xa_k_norm[l], xa_wo[l])
        x = x + peer(rmsnorm(x, ffn_norm[l]), peer_wq[l], peer_keys[l], peer_u[l], peer_v[l])
    return x
```

```python
import functools
import math

import jax
import jax.numpy as jnp
import numpy as np
from jax import lax
from jax.experimental import pallas as pl
from jax.experimental.pallas import tpu as pltpu

F32 = jnp.float32
BF16 = jnp.bfloat16
I32 = jnp.int32
U32 = jnp.uint32

EPS = 1e-6
CHUNK = 64

ML_HEADS, ML_QK, ML_V = 8, 128, 256
ML_WIDTH = ML_HEADS * ML_V
CV_WIDTH, CV_K = 1024, 3
DA_HEADS, DA_QK, DA_V = 8, 64, 128
DA_WIDTH = DA_HEADS * DA_V
XA_HEADS, XA_DIM = 4, 128
XA_WIDTH = XA_HEADS * XA_DIM
PEER_HEADS, N_KEYS, PEER_TOPK = 8, 128, 16
PEER_HALF = 128
PEER_SEL = PEER_HEADS * PEER_TOPK

V7X_VMEM_BYTES = 64 * 1024 * 1024
VMEM_LIMIT = V7X_VMEM_BYTES - 8 * 1024 * 1024
LANES = 128

_NT = (((1,), (1,)), ((), ()))


def _cparams(*sem):
    return pltpu.CompilerParams(dimension_semantics=sem, vmem_limit_bytes=VMEM_LIMIT)


def _rms(x, g):
    return x * lax.rsqrt(jnp.mean(x * x, axis=-1, keepdims=True) + EPS) * g


def _rmsnorm_kernel(x_ref, g_ref, o_ref):
    o_ref[...] = _rms(x_ref[...].astype(F32), g_ref[...]).astype(o_ref.dtype)


def rmsnorm(x, g, *, tr, out_dtype=BF16):
    rows, width = x.shape
    return pl.pallas_call(
        _rmsnorm_kernel,
        grid=(rows // tr,),
        in_specs=[pl.BlockSpec((tr, width), lambda i: (i, 0)), pl.BlockSpec((1, width), lambda i: (0, 0))],
        out_specs=pl.BlockSpec((tr, width), lambda i: (i, 0)),
        out_shape=jax.ShapeDtypeStruct((rows, width), out_dtype),
        compiler_params=_cparams("parallel"),
        name="rmsnorm",
    )(x, g.reshape(1, width))


def _mm_kernel(a_ref, b_ref, o_ref):
    o_ref[...] = jnp.dot(a_ref[...], b_ref[...], preferred_element_type=F32).astype(o_ref.dtype)


def _mm_bias_kernel(a_ref, b_ref, bias_ref, o_ref):
    acc = jnp.dot(a_ref[...], b_ref[...], preferred_element_type=F32)
    o_ref[...] = (acc + bias_ref[...]).astype(o_ref.dtype)


def _mm_res_kernel(a_ref, b_ref, r_ref, o_ref):
    acc = jnp.dot(a_ref[...], b_ref[...], preferred_element_type=F32)
    o_ref[...] = (r_ref[...] + acc).astype(o_ref.dtype)


def matmul(a, b, *, tm, tn, out_dtype, bias=None, residual=None, name="matmul"):
    m, k = a.shape
    n = b.shape[1]
    in_specs = [pl.BlockSpec((tm, k), lambda i, j: (i, 0)), pl.BlockSpec((k, tn), lambda i, j: (0, j))]
    args = [a, b]
    kern = _mm_kernel
    if bias is not None:
        kern = _mm_bias_kernel
        in_specs.append(pl.BlockSpec((1, tn), lambda i, j: (0, j)))
        args.append(bias)
    if residual is not None:
        kern = _mm_res_kernel
        in_specs.append(pl.BlockSpec((tm, tn), lambda i, j: (i, j)))
        args.append(residual)
    return pl.pallas_call(
        kern,
        grid=(m // tm, n // tn),
        in_specs=in_specs,
        out_specs=pl.BlockSpec((tm, tn), lambda i, j: (i, j)),
        out_shape=jax.ShapeDtypeStruct((m, n), out_dtype),
        compiler_params=_cparams("parallel", "arbitrary"),
        name=name,
    )(*args)


def _log_sigmoid(x):
    return jnp.minimum(x, 0.0) - jnp.log(1.0 + jnp.exp(-jnp.abs(x)))


def _mlstm_kernel(q_ref, kt_ref, k_ref, v_ref, o_ref, ifc_ref, ifr_ref, g_ref, out_ref, ct_ref, n_ref, m_ref, *, L):
    c = pl.program_id(0)

    @pl.when(c == 0)
    def _():
        ct_ref[...] = jnp.zeros_like(ct_ref)
        n_ref[...] = jnp.zeros_like(n_ref)
        m_ref[...] = jnp.zeros_like(m_ref)

    row = lax.broadcasted_iota(I32, (L, L), 0)
    col = lax.broadcasted_iota(I32, (L, L), 1)
    tril = col <= row
    ltri = tril.astype(F32)
    utri = (row <= col).astype(F32)
    ifc = ifc_ref[...]
    ifr = ifr_ref[...]
    b_cols = jnp.dot(ltri, _log_sigmoid(ifc), precision=lax.Precision.HIGHEST, preferred_element_type=F32)
    b_rows = jnp.dot(_log_sigmoid(ifr), utri, precision=lax.Precision.HIGHEST, preferred_element_type=F32)
    scale = ML_QK ** -0.5
    for h in range(ML_HEADS):
        q = q_ref[:, h * ML_QK:(h + 1) * ML_QK]
        kt = kt_ref[h * ML_QK:(h + 1) * ML_QK, :]
        k = k_ref[:, h * ML_QK:(h + 1) * ML_QK]
        v = v_ref[:, h * ML_V:(h + 1) * ML_V]
        bc = b_cols[:, ML_HEADS + h:ML_HEADS + h + 1]
        br = b_rows[ML_HEADS + h:ML_HEADS + h + 1, :]
        ic = ifc[:, h:h + 1]
        ir = ifr[h:h + 1, :]
        m_prev = m_ref[h:h + 1, 0:1]
        n_prev = n_ref[h:h + 1, :]
        ct_prev = ct_ref[h]

        d = jnp.where(tril, bc - br + ir, -jnp.inf)
        inter = bc + m_prev
        mt = jnp.maximum(inter, jnp.max(d, axis=1, keepdims=True))
        qk = jnp.dot(q, kt, preferred_element_type=F32) * scale
        w = jnp.exp(d - mt) * qk
        s_prev = jnp.exp(inter - mt)
        num = jnp.dot(w.astype(BF16), v, preferred_element_type=F32)
        num = num + s_prev * (jnp.dot(q, ct_prev.astype(BF16), preferred_element_type=F32) * scale)
        qn = jnp.sum(q.astype(F32) * n_prev, axis=1, keepdims=True) * scale
        den = jnp.sum(w, axis=1, keepdims=True) + s_prev * qn
        hh = num / jnp.maximum(jnp.abs(den), jnp.exp(-mt))

        bl = bc[L - 1:L, :]
        a_col = bl - bc + ic
        a_row = bl - br + ir
        m_new = jnp.maximum(bl + m_prev, jnp.max(a_row, axis=1, keepdims=True))
        decay = jnp.exp(bl + m_prev - m_new)
        wa = jnp.exp(a_col - m_new)
        vf = v.astype(F32)
        ct_ref[h] = decay * ct_prev + jnp.dot(kt, (wa * vf).astype(BF16), preferred_element_type=F32)
        n_ref[h:h + 1, :] = decay * n_prev + jnp.sum(wa * k.astype(F32), axis=0, keepdims=True)
        m_ref[h:h + 1, :] = jnp.broadcast_to(m_new, (1, LANES))

        gain = g_ref[:, h * ML_V:(h + 1) * ML_V]
        og = o_ref[:, h * ML_V:(h + 1) * ML_V].astype(F32)
        out_ref[:, h * ML_V:(h + 1) * ML_V] = (_rms(hh, gain) * jax.nn.sigmoid(og)).astype(out_ref.dtype)


def mlstm(proj, kt, if_col, if_row, gain, *, L):
    s = proj.shape[0]
    qk_w = ML_HEADS * ML_QK
    return pl.pallas_call(
        functools.partial(_mlstm_kernel, L=L),
        grid=(s // L,),
        in_specs=[
            pl.BlockSpec((L, qk_w), lambda c: (c, 0)),
            pl.BlockSpec((qk_w, L), lambda c: (0, c)),
            pl.BlockSpec((L, qk_w), lambda c: (c, 1)),
            pl.BlockSpec((L, ML_WIDTH), lambda c: (c, 1)),
            pl.BlockSpec((L, ML_WIDTH), lambda c: (c, 2)),
            pl.BlockSpec((L, LANES), lambda c: (c, 0)),
            pl.BlockSpec((2 * ML_HEADS, L), lambda c: (0, c)),
            pl.BlockSpec((1, ML_WIDTH), lambda c: (0, 0)),
        ],
        out_specs=pl.BlockSpec((L, ML_WIDTH), lambda c: (c, 0)),
        out_shape=jax.ShapeDtypeStruct((s, ML_WIDTH), BF16),
        scratch_shapes=[
            pltpu.VMEM((ML_HEADS, ML_QK, ML_V), F32),
            pltpu.VMEM((ML_HEADS, ML_QK), F32),
            pltpu.VMEM((ML_HEADS, LANES), F32),
        ],
        compiler_params=_cparams("arbitrary"),
        name="mlstm",
    )(proj, kt, proj, proj, proj, if_col, if_row, gain.reshape(1, ML_WIDTH))


def _conv_kernel(b_ref, c_ref, h_ref, w_ref, o_ref, carry_ref):
    @pl.when(pl.program_id(0) == 0)
    def _():
        carry_ref[...] = jnp.zeros_like(carry_ref)

    u = c_ref[...].astype(F32) * h_ref[...].astype(F32)
    rows = u.shape[0]
    ext = jnp.concatenate([carry_ref[...], u], axis=0)
    u1 = ext[7:7 + rows]
    u2 = ext[6:6 + rows]
    w = w_ref[...]
    y = w[0:1] * u2 + w[1:2] * u1 + w[2:3] * u
    o_ref[...] = (b_ref[...].astype(F32) * y).astype(o_ref.dtype)
    carry_ref[...] = u[rows - 8:rows]


def short_conv(proj, w, *, tr):
    s = proj.shape[0]
    base = (2 * ML_HEADS * ML_QK + 2 * ML_WIDTH) // CV_WIDTH
    return pl.pallas_call(
        _conv_kernel,
        grid=(s // tr,),
        in_specs=[
            pl.BlockSpec((tr, CV_WIDTH), lambda i: (i, base)),
            pl.BlockSpec((tr, CV_WIDTH), lambda i: (i, base + 1)),
            pl.BlockSpec((tr, CV_WIDTH), lambda i: (i, base + 2)),
            pl.BlockSpec((CV_K, CV_WIDTH), lambda i: (0, 0)),
        ],
        out_specs=pl.BlockSpec((tr, CV_WIDTH), lambda i: (i, 0)),
        out_shape=jax.ShapeDtypeStruct((s, CV_WIDTH), BF16),
        scratch_shapes=[pltpu.VMEM((8, CV_WIDTH), F32)],
        compiler_params=_cparams("arbitrary"),
        name="short_conv",
    )(proj, proj, proj, w)


def _da_prep_kernel(q_ref, k_ref, qg_ref, kg_ref, qo_ref, ko_ref):
    r = lax.broadcasted_iota(I32, (LANES, LANES), 0) // DA_QK
    c = lax.broadcasted_iota(I32, (LANES, LANES), 1) // DA_QK
    seg = (r == c).astype(F32)

    def norm(x_ref, g_ref, o_ref, scale):
        for j in range(x_ref.shape[1] // LANES):
            x = x_ref[:, j * LANES:(j + 1) * LANES].astype(F32)
            ss = jnp.dot(x * x, seg, precision=lax.Precision.HIGHEST, preferred_element_type=F32)
            y = x * lax.rsqrt(ss * (1.0 / DA_QK) + EPS) * g_ref[:, j * LANES:(j + 1) * LANES]
            o_ref[:, j * LANES:(j + 1) * LANES] = (y * scale).astype(o_ref.dtype)

    norm(q_ref, qg_ref, qo_ref, DA_QK ** -0.5)
    norm(k_ref, kg_ref, ko_ref, 1.0)


def da_prep(proj, q_g, k_g, *, tr):
    s = proj.shape[0]
    w = DA_HEADS * 2 * DA_QK
    base = (2 * ML_HEADS * ML_QK + 2 * ML_WIDTH + 3 * CV_WIDTH) // w
    qg = jnp.tile(q_g, w // DA_QK).reshape(1, w)
    kg = jnp.tile(k_g, w // DA_QK).reshape(1, w)
    return pl.pallas_call(
        _da_prep_kernel,
        grid=(s // tr,),
        in_specs=[
            pl.BlockSpec((tr, w), lambda i: (i, base)),
            pl.BlockSpec((tr, w), lambda i: (i, base + 1)),
            pl.BlockSpec((1, w), lambda i: (0, 0)),
            pl.BlockSpec((1, w), lambda i: (0, 0)),
        ],
        out_specs=[pl.BlockSpec((tr, w), lambda i: (i, 0)), pl.BlockSpec((tr, w), lambda i: (i, 0))],
        out_shape=[jax.ShapeDtypeStruct((s, w), BF16), jax.ShapeDtypeStruct((s, w), BF16)],
        compiler_params=_cparams("parallel"),
        name="da_prep",
    )(proj, proj, qg, kg)


def _da_kernel(q_ref, k_ref, v_ref, lam_ref, g_ref, o_ref, *, tq, lam_init):
    qi = pl.program_id(1)
    q = q_ref[...]
    lane = lax.broadcasted_iota(I32, q.shape, 1)
    zero = jnp.zeros_like(q)
    q1 = jnp.where(lane < DA_QK, q, zero)
    q2 = jnp.where(lane >= DA_QK, q, zero)

    def scores(kb):
        return (lax.dot_general(q1, kb, _NT, preferred_element_type=F32),
                lax.dot_general(q2, kb, _NT, preferred_element_type=F32))

    def update(carry, s, vb):
        m, l, acc = carry
        m_new = jnp.maximum(m, jnp.max(s, axis=1, keepdims=True))
        alpha = jnp.exp(m - m_new)
        p = jnp.exp(s - m_new)
        l = alpha * l + jnp.sum(p, axis=1, keepdims=True)
        acc = alpha * acc + jnp.dot(p.astype(BF16), vb, preferred_element_type=F32)
        return m_new, l, acc

    def body(ki, carry):
        c1, c2 = carry
        off = pl.multiple_of(ki * tq, tq)
        kb = k_ref[pl.ds(off, tq), :]
        vb = v_ref[pl.ds(off, tq), :]
        s1, s2 = scores(kb)
        return update(c1, s1, vb), update(c2, s2, vb)

    def init():
        return (jnp.full((tq, 1), -jnp.inf, F32), jnp.zeros((tq, 1), F32), jnp.zeros((tq, DA_V), F32))

    c1, c2 = lax.fori_loop(0, qi, body, (init(), init()))
    off = pl.multiple_of(qi * tq, tq)
    kb = k_ref[pl.ds(off, tq), :]
    vb = v_ref[pl.ds(off, tq), :]
    s1, s2 = scores(kb)
    rq = lax.broadcasted_iota(I32, (tq, tq), 0) // CHUNK
    ck = lax.broadcasted_iota(I32, (tq, tq), 1) // CHUNK
    vis = ck <= rq
    _, l1, a1 = update(c1, jnp.where(vis, s1, -jnp.inf), vb)
    _, l2, a2 = update(c2, jnp.where(vis, s2, -jnp.inf), vb)

    lp = lam_ref[...]
    lam = (jnp.exp(jnp.sum(lp[0:1] * lp[1:2], axis=1, keepdims=True))
           - jnp.exp(jnp.sum(lp[2:3] * lp[3:4], axis=1, keepdims=True)) + lam_init)
    o = a1 / l1 - lam * (a2 / l2)
    o_ref[...] = (_rms(o, g_ref[...]) * (1.0 - lam_init)).astype(o_ref.dtype)


def diff_attention(qn, kn, proj, lam_p, subln_g, *, tq, lam_init):
    s = qn.shape[0]
    vbase = (2 * ML_HEADS * ML_QK + 2 * ML_WIDTH + 3 * CV_WIDTH + 2 * DA_HEADS * 2 * DA_QK) // DA_V
    return pl.pallas_call(
        functools.partial(_da_kernel, tq=tq, lam_init=lam_init),
        grid=(DA_HEADS, s // tq),
        in_specs=[
            pl.BlockSpec((tq, 2 * DA_QK), lambda h, i: (i, h)),
            pl.BlockSpec((s, 2 * DA_QK), lambda h, i: (0, h)),
            pl.BlockSpec((s, DA_V), lambda h, i: (0, vbase + h)),
            pl.BlockSpec((4, DA_QK), lambda h, i: (0, 0)),
            pl.BlockSpec((1, DA_V), lambda h, i: (0, 0)),
        ],
        out_specs=pl.BlockSpec((tq, DA_V), lambda h, i: (i, h)),
        out_shape=jax.ShapeDtypeStruct((s, DA_WIDTH), BF16),
        compiler_params=_cparams("parallel", "arbitrary"),
        name="diff_attention",
    )(qn, kn, proj, lam_p, subln_g.reshape(1, DA_V))


def _merge_kernel(h_ref, hm_ref, hc_ref, hd_ref, wg0, wg1, wg2, bg0, bg1, bg2, wbm, wbc, wbd, o_ref):
    h = h_ref[...]

    def gated(wg, bg, hb_ref, wb):
        gate = jax.nn.sigmoid(jnp.dot(h, wg[...], preferred_element_type=F32) + bg[...])
        return gate * jnp.dot(hb_ref[...], wb[...], preferred_element_type=F32)

    out = gated(wg0, bg0, hm_ref, wbm) + gated(wg1, bg1, hc_ref, wbc) + gated(wg2, bg2, hd_ref, wbd)
    o_ref[...] = out.astype(o_ref.dtype)


def gated_merge(h, hm, hc, hd, w_gate, b_gate, w_branch, *, tm, tn):
    s, d = h.shape
    nj = d // tn
    r_c = ML_WIDTH // CV_WIDTH
    r_d = (ML_WIDTH + CV_WIDTH) // DA_WIDTH

    def gspec(b):
        return pl.BlockSpec((d, tn), lambda i, j: (0, j + b * nj))

    def bspec(b):
        return pl.BlockSpec((1, tn), lambda i, j: (0, j + b * nj))

    return pl.pallas_call(
        _merge_kernel,
        grid=(s // tm, nj),
        in_specs=[
            pl.BlockSpec((tm, d), lambda i, j: (i, 0)),
            pl.BlockSpec((tm, ML_WIDTH), lambda i, j: (i, 0)),
            pl.BlockSpec((tm, CV_WIDTH), lambda i, j: (i, 0)),
            pl.BlockSpec((tm, DA_WIDTH), lambda i, j: (i, 0)),
            gspec(0), gspec(1), gspec(2), bspec(0), bspec(1), bspec(2),
            pl.BlockSpec((ML_WIDTH, tn), lambda i, j: (0, j)),
            pl.BlockSpec((CV_WIDTH, tn), lambda i, j: (r_c, j)),
            pl.BlockSpec((DA_WIDTH, tn), lambda i, j: (r_d, j)),
        ],
        out_specs=pl.BlockSpec((tm, tn), lambda i, j: (i, j)),
        out_shape=jax.ShapeDtypeStruct((s, d), BF16),
        compiler_params=_cparams("parallel", "arbitrary"),
        name="gated_merge",
    )(h, hm, hc, hd, w_gate, w_gate, w_gate, b_gate, b_gate, b_gate, w_branch, w_branch, w_branch)


def _xa_kv_kernel(mem_ref, mg_ref, wkv_ref, kg_ref, k_ref, v_ref):
    mn = _rms(mem_ref[...], mg_ref[...]).astype(BF16)
    kv = jnp.dot(mn, wkv_ref[...], preferred_element_type=F32)
    for hh in range(XA_HEADS):
        kh = kv[:, hh * XA_DIM:(hh + 1) * XA_DIM]
        k_ref[:, hh * XA_DIM:(hh + 1) * XA_DIM] = _rms(kh, kg_ref[...]).astype(k_ref.dtype)
    v_ref[...] = kv[:, XA_WIDTH:].astype(v_ref.dtype)


def xa_kv(mem, mem_g, wkv, k_g):
    m, d = mem.shape
    return pl.pallas_call(
        _xa_kv_kernel,
        out_shape=[jax.ShapeDtypeStruct((m, XA_WIDTH), BF16), jax.ShapeDtypeStruct((m, XA_WIDTH), BF16)],
        compiler_params=pltpu.CompilerParams(vmem_limit_bytes=VMEM_LIMIT),
        name="xa_kv",
    )(mem, mem_g.reshape(1, d), wkv, k_g.reshape(1, XA_DIM))


def _xa_kernel(x_ref, ng_ref, wq_ref, qg_ref, k_ref, v_ref, wo_ref, fg_ref, y_ref, yn_ref):
    x = x_ref[...]
    xn = _rms(x, ng_ref[...]).astype(BF16)
    q = jnp.dot(xn, wq_ref[...], preferred_element_type=F32)
    outs = []
    for hh in range(XA_HEADS):
        sl = slice(hh * XA_DIM, (hh + 1) * XA_DIM)
        qh = _rms(q[:, sl], qg_ref[...]).astype(BF16)
        s = lax.dot_general(qh, k_ref[:, sl], _NT, preferred_element_type=F32) * (XA_DIM ** -0.5)
        p = jnp.exp(s - jnp.max(s, axis=1, keepdims=True))
        p = p / jnp.sum(p, axis=1, keepdims=True)
        outs.append(jnp.dot(p.astype(BF16), v_ref[:, sl], preferred_element_type=F32))
    o = jnp.concatenate(outs, axis=1).astype(BF16)
    y = x + jnp.dot(o, wo_ref[...], preferred_element_type=F32)
    y_ref[...] = y
    yn_ref[...] = _rms(y, fg_ref[...]).astype(yn_ref.dtype)


def cross_attention(x, norm_g, wq, q_g, k, v, wo, next_g, *, tm):
    s, d = x.shape
    m = k.shape[0]
    const = lambda i: (0, 0)
    return pl.pallas_call(
        _xa_kernel,
        grid=(s // tm,),
        in_specs=[
            pl.BlockSpec((tm, d), lambda i: (i, 0)),
            pl.BlockSpec((1, d), const),
            pl.BlockSpec((d, XA_WIDTH), const),
            pl.BlockSpec((1, XA_DIM), const),
            pl.BlockSpec((m, XA_WIDTH), const),
            pl.BlockSpec((m, XA_WIDTH), const),
            pl.BlockSpec((XA_WIDTH, d), const),
            pl.BlockSpec((1, d), const),
        ],
        out_specs=[pl.BlockSpec((tm, d), lambda i: (i, 0)), pl.BlockSpec((tm, d), lambda i: (i, 0))],
        out_shape=[jax.ShapeDtypeStruct((s, d), F32), jax.ShapeDtypeStruct((s, d), BF16)],
        compiler_params=_cparams("parallel"),
        name="cross_attention",
    )(x, norm_g.reshape(1, d), wq, q_g.reshape(1, XA_DIM), k, v, wo, next_g.reshape(1, d))


def _top16(s, payload=None):
    n = s.shape[0]
    iota = lax.broadcasted_iota(I32, s.shape, 0)
    vals, sel = [], []
    for _ in range(PEER_TOPK):
        mx = jnp.max(s, axis=0, keepdims=True)
        ix = jnp.min(jnp.where(s == mx, iota, n), axis=0, keepdims=True)
        hit = iota == ix
        vals.append(mx)
        if payload is None:
            sel.append(ix)
        else:
            sel.append(jnp.max(jnp.where(hit, payload, -1), axis=0, keepdims=True))
        s = jnp.where(hit, -jnp.inf, s)
    return jnp.concatenate(vals, axis=0), jnp.concatenate(sel, axis=0)


def _peer_topk_kernel(q_ref, keys_ref, idx_ref, g_ref):
    q = q_ref[...].astype(BF16)
    k1 = keys_ref[0, 0].astype(BF16)
    k2 = keys_ref[0, 1].astype(BF16)
    s1 = lax.dot_general(k1, q[:, :PEER_HALF], _NT, preferred_element_type=F32)
    s2 = lax.dot_general(k2, q[:, PEER_HALF:], _NT, preferred_element_type=F32)
    t1, i1 = _top16(s1)
    t2, i2 = _top16(s2)
    cand = jnp.concatenate([t1[a:a + 1] + t2 for a in range(PEER_TOPK)], axis=0)
    cidx = jnp.concatenate([i1[a:a + 1] * N_KEYS + i2 for a in range(PEER_TOPK)], axis=0)
    top, eidx = _top16(cand, cidx)
    e = jnp.exp(top - top[0:1])
    idx_ref[0] = eidx
    g_ref[0] = e / jnp.sum(e, axis=0, keepdims=True)


def peer_topk(q, keys, *, tt):
    s = q.shape[0]
    return pl.pallas_call(
        _peer_topk_kernel,
        grid=(s // tt, PEER_HEADS),
        in_specs=[
            pl.BlockSpec((tt, 2 * PEER_HALF), lambda i, h: (i, h)),
            pl.BlockSpec((1, 2, N_KEYS, PEER_HALF), lambda i, h: (h, 0, 0, 0)),
        ],
        out_specs=[pl.BlockSpec((1, PEER_TOPK, tt), lambda i, h: (h, 0, i)),
                   pl.BlockSpec((1, PEER_TOPK, tt), lambda i, h: (h, 0, i))],
        out_shape=[jax.ShapeDtypeStruct((PEER_HEADS, PEER_TOPK, s), I32),
                   jax.ShapeDtypeStruct((PEER_HEADS, PEER_TOPK, s), F32)],
        compiler_params=_cparams("parallel", "parallel"),
        name="peer_topk",
    )(q, keys)


PEER_NBUF = 4


def _peer_mix_kernel(idx_ref, g2_ref, x_ref, ng_ref, tab_ref, o_ref, idx_smem, xn_ref, buf, sems, idx_sem, *, tb, half):
    cp = pltpu.make_async_copy(idx_ref, idx_smem, idx_sem)
    cp.start()
    xn_ref[...] = _rms(x_ref[...], ng_ref[...])
    cp.wait()

    def row_copy(t, e, slot):
        return pltpu.make_async_copy(tab_ref.at[pl.ds(idx_smem[t, e], 1), :], buf.at[slot, pl.ds(e, 1), :], sems.at[slot])

    def issue(t, slot):
        for e in range(PEER_SEL):
            row_copy(t, e, slot).start(priority=e % 2)

    def wait(slot):
        pltpu.make_async_copy(tab_ref.at[pl.ds(0, PEER_SEL), :], buf.at[slot], sems.at[slot]).wait()

    for t in range(PEER_NBUF - 1):
        issue(t, t)

    lane = lax.broadcasted_iota(I32, (8, 2 * PEER_SEL), 1)
    sub = lax.broadcasted_iota(I32, (8, 2 * PEER_SEL), 0)
    even = (lane % 2) == 0
    pick = ((sub == 0) & even) | ((sub == 1) & jnp.logical_not(even))

    def body(t, _):
        slot = t % PEER_NBUF

        @pl.when(t + PEER_NBUF - 1 < tb)
        def _():
            issue(t + PEER_NBUF - 1, (t + PEER_NBUF - 1) % PEER_NBUF)

        wait(slot)
        xrow = xn_ref[pl.ds(t, 1), :]
        x8 = jnp.concatenate([xrow[:, :half], xrow[:, half:], jnp.zeros((6, half), F32)], axis=0).astype(BF16)
        ub = pltpu.bitcast(buf[slot, :, :half], BF16)
        vb = pltpu.bitcast(buf[slot, :, half:], BF16)
        r = lax.dot_general(x8, ub, _NT, preferred_element_type=F32)
        a = jnp.where(even, r[0:1], r[1:2])
        a = jnp.broadcast_to(a, (8, 2 * PEER_SEL))
        act = a + jnp.where(even, pltpu.roll(a, 2 * PEER_SEL - 1, 1), pltpu.roll(a, 1, 1))
        wgt = jax.nn.gelu(act) * g2_ref[pl.ds(t, 1), :]
        w8 = jnp.where(pick, wgt, 0.0).astype(BF16)
        o = jnp.dot(w8, vb, preferred_element_type=F32)
        o_ref[pl.ds(t, 1), :] = x_ref[pl.ds(t, 1), :] + jnp.concatenate([o[0:1], o[1:2]], axis=1)
        return 0

    lax.fori_loop(0, tb, body, 0)


def peer_mix(eidx, g2, x, norm_g, table, *, tb):
    s, d = x.shape
    half = d // 2
    return pl.pallas_call(
        functools.partial(_peer_mix_kernel, tb=tb, half=half),
        grid=(s // tb,),
        in_specs=[
            pl.BlockSpec((tb, PEER_SEL), lambda i: (i, 0)),
            pl.BlockSpec((tb, 2 * PEER_SEL), lambda i: (i, 0)),
            pl.BlockSpec((tb, d), lambda i: (i, 0)),
            pl.BlockSpec((1, d), lambda i: (0, 0)),
            pl.BlockSpec(memory_space=pl.ANY),
        ],
        out_specs=pl.BlockSpec((tb, d), lambda i: (i, 0)),
        out_shape=jax.ShapeDtypeStruct((s, d), F32),
        scratch_shapes=[
            pltpu.SMEM((tb, PEER_SEL), I32),
            pltpu.VMEM((tb, d), F32),
            pltpu.VMEM((PEER_NBUF, PEER_SEL, d), U32),
            pltpu.SemaphoreType.DMA((PEER_NBUF,)),
            pltpu.SemaphoreType.DMA,
        ],
        compiler_params=_cparams("arbitrary"),
        name="peer_mix",
    )(eidx, g2, x, norm_g.reshape(1, d), table)


def _pack_expert_table(u, v):
    def pack(w):
        half = w.shape[1] // 2
        b = lax.bitcast_convert_type(w.astype(BF16), jnp.uint16).astype(U32)
        return b[:, :half] | (b[:, half:] << 16)
    return jnp.concatenate([pack(u), pack(v)], axis=1)


def _layer(x, h, mem, lam_init, p):
    s, d = x.shape
    n_qkvo = 2 * ML_HEADS * ML_QK + 2 * ML_WIDTH
    proj = matmul(h, p["w_main"], tm=1024, tn=1024, out_dtype=BF16, name="in_proj")
    gates_if = matmul(h, p["w_if"], tm=1024, tn=LANES, out_dtype=F32, bias=p["b_if"], name="if_proj")
    kt = proj[:, ML_HEADS * ML_QK:2 * ML_HEADS * ML_QK].T
    if_row = gates_if[:, :2 * ML_HEADS].T
    hm = mlstm(proj, kt, gates_if, if_row, p["ml_norm"], L=128)
    hc = short_conv(proj, p["conv_w"], tr=512)
    qn, kn = da_prep(proj, p["da_q_norm"], p["da_k_norm"], tr=512)
    hd = diff_attention(qn, kn, proj, p["da_lambda"], p["da_subln"], tq=512, lam_init=lam_init)
    merged = gated_merge(h, hm, hc, hd, p["w_gate"], p["b_gate"], p["w_branch"], tm=512, tn=256)
    x = matmul(merged, p["w_out"], tm=512, tn=1024, out_dtype=F32, residual=x, name="out_proj")
    k, v = xa_kv(mem, p["mem_norm"], p["xa_wkv"], p["xa_k_norm"])
    x, xn = cross_attention(x, p["xa_norm"], p["xa_wq"], p["xa_q_norm"], k, v, p["xa_wo"], p["ffn_norm"], tm=256)
    q = matmul(xn, p["peer_wq"], tm=1024, tn=1024, out_dtype=F32, name="peer_q")
    eidx, g = peer_topk(q, p["peer_keys"], tt=512)
    eidx = jnp.transpose(eidx, (2, 0, 1)).reshape(s, PEER_SEL)
    g2 = jnp.repeat(jnp.transpose(g, (2, 0, 1)).reshape(s, PEER_SEL), 2, axis=1)
    return peer_mix(eidx, g2, x, p["ffn_norm"], p["table"], tb=128)


def kernel(x, mem, mix_norm, w_in, b_if, ml_norm, conv_w, da_q_norm, da_k_norm, da_lambda, da_subln, w_gate, b_gate, w_branch, w_out, xa_norm, mem_norm, xa_wq, xa_wkv, xa_q_norm, xa_k_norm, xa_wo, ffn_norm, peer_wq, peer_keys, peer_u, peer_v):
    depth = w_in.shape[0]
    d = x.shape[-1]
    xs = x.reshape(-1, d)
    mems = mem.reshape(-1, d)
    n_qkvo = 2 * ML_HEADS * ML_QK + 2 * ML_WIDTH
    for l in range(depth):
        wl = w_in[l]
        w_if = jnp.pad(wl[:, n_qkvo:n_qkvo + 2 * ML_HEADS], ((0, 0), (0, LANES - 2 * ML_HEADS)))
        bias_if = jnp.pad(b_if[l].reshape(1, 2 * ML_HEADS), ((0, 0), (0, LANES - 2 * ML_HEADS)))
        p = dict(
            w_main=jnp.concatenate([wl[:, :n_qkvo], wl[:, n_qkvo + 2 * ML_HEADS:]], axis=1).astype(BF16),
            w_if=w_if.astype(BF16), b_if=bias_if,
            ml_norm=ml_norm[l], conv_w=conv_w[l], da_q_norm=da_q_norm[l], da_k_norm=da_k_norm[l],
            da_lambda=da_lambda[l], da_subln=da_subln[l],
            w_gate=w_gate[l].astype(BF16), b_gate=b_gate[l].reshape(1, -1), w_branch=w_branch[l].astype(BF16),
            w_out=w_out[l].astype(BF16), xa_norm=xa_norm[l], mem_norm=mem_norm[l], xa_wq=xa_wq[l].astype(BF16),
            xa_wkv=xa_wkv[l].astype(BF16), xa_q_norm=xa_q_norm[l], xa_k_norm=xa_k_norm[l],
            xa_wo=xa_wo[l].astype(BF16), ffn_norm=ffn_norm[l], peer_wq=peer_wq[l].astype(BF16),
            peer_keys=peer_keys[l], table=_pack_expert_table(peer_u[l], peer_v[l]),
        )
        h = rmsnorm(xs, mix_norm[l], tr=256)
        lam_init = 0.8 - 0.6 * math.exp(-0.3 * l)
        xs = _layer(xs, h, mems, lam_init, p)
    return xs.reshape(x.shape)
```

```python
import functools
import math

import jax
import jax.numpy as jnp
import numpy as np
from jax import lax
from jax.experimental import pallas as pl
from jax.experimental.pallas import tpu as pltpu

F32 = jnp.float32
BF16 = jnp.bfloat16
I32 = jnp.int32
U32 = jnp.uint32

EPS = 1e-6
CHUNK = 64

ML_HEADS, ML_QK, ML_V = 8, 128, 256
ML_WIDTH = ML_HEADS * ML_V
CV_WIDTH, CV_K = 1024, 3
DA_HEADS, DA_QK, DA_V = 8, 64, 128
DA_WIDTH = DA_HEADS * DA_V
XA_HEADS, XA_DIM = 4, 128
XA_WIDTH = XA_HEADS * XA_DIM
PEER_HEADS, N_KEYS, PEER_TOPK = 8, 128, 16
PEER_HALF = 128
PEER_SEL = PEER_HEADS * PEER_TOPK

V7X_VMEM_BYTES = 64 * 1024 * 1024
VMEM_LIMIT = V7X_VMEM_BYTES - 8 * 1024 * 1024
LANES = 128

_NT = (((1,), (1,)), ((), ()))


def _cparams(*sem):
    return pltpu.CompilerParams(dimension_semantics=sem, vmem_limit_bytes=VMEM_LIMIT)


def _rms(x, g):
    return x * lax.rsqrt(jnp.mean(x * x, axis=-1, keepdims=True) + EPS) * g


def _rmsnorm_kernel(x_ref, g_ref, o_ref):
    o_ref[...] = _rms(x_ref[...].astype(F32), g_ref[...]).astype(o_ref.dtype)


def rmsnorm(x, g, *, tr, out_dtype=BF16):
    rows, width = x.shape
    return pl.pallas_call(
        _rmsnorm_kernel,
        grid=(rows // tr,),
        in_specs=[pl.BlockSpec((tr, width), lambda i: (i, 0)), pl.BlockSpec((1, width), lambda i: (0, 0))],
        out_specs=pl.BlockSpec((tr, width), lambda i: (i, 0)),
        out_shape=jax.ShapeDtypeStruct((rows, width), out_dtype),
        compiler_params=_cparams("parallel"),
        name="rmsnorm",
    )(x, g.reshape(1, width))


def _mm_kernel(a_ref, b_ref, o_ref):
    o_ref[...] = jnp.dot(a_ref[...], b_ref[...], preferred_element_type=F32).astype(o_ref.dtype)


def _mm_bias_kernel(a_ref, b_ref, bias_ref, o_ref):
    acc = jnp.dot(a_ref[...], b_ref[...], preferred_element_type=F32)
    o_ref[...] = (acc + bias_ref[...]).astype(o_ref.dtype)


def _mm_res_kernel(a_ref, b_ref, r_ref, o_ref):
    acc = jnp.dot(a_ref[...], b_ref[...], preferred_element_type=F32)
    o_ref[...] = (r_ref[...] + acc).astype(o_ref.dtype)


def matmul(a, b, *, tm, tn, out_dtype, bias=None, residual=None, name="matmul"):
    m, k = a.shape
    n = b.shape[1]
    in_specs = [pl.BlockSpec((tm, k), lambda i, j: (i, 0)), pl.BlockSpec((k, tn), lambda i, j: (0, j))]
    args = [a, b]
    kern = _mm_kernel
    if bias is not None:
        kern = _mm_bias_kernel
        in_specs.append(pl.BlockSpec((1, tn), lambda i, j: (0, j)))
        args.append(bias)
    if residual is not None:
        kern = _mm_res_kernel
        in_specs.append(pl.BlockSpec((tm, tn), lambda i, j: (i, j)))
        args.append(residual)
    return pl.pallas_call(
        kern,
        grid=(m // tm, n // tn),
        in_specs=in_specs,
        out_specs=pl.BlockSpec((tm, tn), lambda i, j: (i, j)),
        out_shape=jax.ShapeDtypeStruct((m, n), out_dtype),
        compiler_params=_cparams("parallel", "arbitrary"),
        name=name,
    )(*args)


def _log_sigmoid(x):
    return jnp.minimum(x, 0.0) - jnp.log(1.0 + jnp.exp(-jnp.abs(x)))


def _mlstm_kernel(q_ref, kt_ref, k_ref, v_ref, o_ref, ifc_ref, ifr_ref, g_ref, out_ref, ct_ref, n_ref, m_ref, *, L):
    c = pl.program_id(0)

    @pl.when(c == 0)
    def _():
        ct_ref[...] = jnp.zeros_like(ct_ref)
        n_ref[...] = jnp.zeros_like(n_ref)
        m_ref[...] = jnp.zeros_like(m_ref)

    row = lax.broadcasted_iota(I32, (L, L), 0)
    col = lax.broadcasted_iota(I32, (L, L), 1)
    tril = col <= row
    ltri = tril.astype(F32)
    utri = (row <= col).astype(F32)
    ifc = ifc_ref[...]
    ifr = ifr_ref[...]
    b_cols = jnp.dot(ltri, _log_sigmoid(ifc), precision=lax.Precision.HIGHEST, preferred_element_type=F32)
    b_rows = jnp.dot(_log_sigmoid(ifr), utri, precision=lax.Precision.HIGHEST, preferred_element_type=F32)
    scale = ML_QK ** -0.5
    for h in range(ML_HEADS):
        q = q_ref[:, h * ML_QK:(h + 1) * ML_QK]
        kt = kt_ref[h * ML_QK:(h + 1) * ML_QK, :]
        k = k_ref[:, h * ML_QK:(h + 1) * ML_QK]
        v = v_ref[:, h * ML_V:(h + 1) * ML_V]
        bc = b_cols[:, ML_HEADS + h:ML_HEADS + h + 1]
        br = b_rows[ML_HEADS + h:ML_HEADS + h + 1, :]
        ic = ifc[:, h:h + 1]
        ir = ifr[h:h + 1, :]
        m_prev = m_ref[h:h + 1, 0:1]
        n_prev = n_ref[h:h + 1, :]
        ct_prev = ct_ref[h]

        d = jnp.where(tril, bc - br + ir, -jnp.inf)
        inter = bc + m_prev
        mt = jnp.maximum(inter, jnp.max(d, axis=1, keepdims=True))
        qk = jnp.dot(q, kt, preferred_element_type=F32) * scale
        w = jnp.exp(d - mt) * qk
        s_prev = jnp.exp(inter - mt)
        num = jnp.dot(w.astype(BF16), v, preferred_element_type=F32)
        num = num + s_prev * (jnp.dot(q, ct_prev.astype(BF16), preferred_element_type=F32) * scale)
        qn = jnp.sum(q.astype(F32) * n_prev, axis=1, keepdims=True) * scale
        den = jnp.sum(w, axis=1, keepdims=True) + s_prev * qn
        hh = num / jnp.maximum(jnp.abs(den), jnp.exp(-mt))

        bl = bc[L - 1:L, :]
        a_col = bl - bc + ic
        a_row = bl - br + ir
        m_new = jnp.maximum(bl + m_prev, jnp.max(a_row, axis=1, keepdims=True))
        decay = jnp.exp(bl + m_prev - m_new)
        wa = jnp.exp(a_col - m_new)
        vf = v.astype(F32)
        ct_ref[h] = decay * ct_prev + jnp.dot(kt, (wa * vf).astype(BF16), preferred_element_type=F32)
        n_ref[h:h + 1, :] = decay * n_prev + jnp.sum(wa * k.astype(F32), axis=0, keepdims=True)
        m_ref[h:h + 1, :] = jnp.broadcast_to(m_new, (1, LANES))

        gain = g_ref[:, h * ML_V:(h + 1) * ML_V]
        og = o_ref[:, h * ML_V:(h + 1) * ML_V].astype(F32)
        out_ref[:, h * ML_V:(h + 1) * ML_V] = (_rms(hh, gain) * jax.nn.sigmoid(og)).astype(out_ref.dtype)


def mlstm(proj, kt, if_col, if_row, gain, *, L):
    s = proj.shape[0]
    qk_w = ML_HEADS * ML_QK
    return pl.pallas_call(
        functools.partial(_mlstm_kernel, L=L),
        grid=(s // L,),
        in_specs=[
            pl.BlockSpec((L, qk_w), lambda c: (c, 0)),
            pl.BlockSpec((qk_w, L), lambda c: (0, c)),
            pl.BlockSpec((L, qk_w), lambda c: (c, 1)),
            pl.BlockSpec((L, ML_WIDTH), lambda c: (c, 1)),
            pl.BlockSpec((L, ML_WIDTH), lambda c: (c, 2)),
            pl.BlockSpec((L, LANES), lambda c: (c, 0)),
            pl.BlockSpec((2 * ML_HEADS, L), lambda c: (0, c)),
            pl.BlockSpec((1, ML_WIDTH), lambda c: (0, 0)),
        ],
        out_specs=pl.BlockSpec((L, ML_WIDTH), lambda c: (c, 0)),
        out_shape=jax.ShapeDtypeStruct((s, ML_WIDTH), BF16),
        scratch_shapes=[
            pltpu.VMEM((ML_HEADS, ML_QK, ML_V), F32),
            pltpu.VMEM((ML_HEADS, ML_QK), F32),
            pltpu.VMEM((ML_HEADS, LANES), F32),
        ],
        compiler_params=_cparams("arbitrary"),
        name="mlstm",
    )(proj, kt, proj, proj, proj, if_col, if_row, gain.reshape(1, ML_WIDTH))


def _conv_kernel(b_ref, c_ref, h_ref, w_ref, o_ref, carry_ref):
    @pl.when(pl.program_id(0) == 0)
    def _():
        carry_ref[...] = jnp.zeros_like(carry_ref)

    u = c_ref[...].astype(F32) * h_ref[...].astype(F32)
    rows = u.shape[0]
    ext = jnp.concatenate([carry_ref[...], u], axis=0)
    u1 = ext[7:7 + rows]
    u2 = ext[6:6 + rows]
    w = w_ref[...]
    y = w[0:1] * u2 + w[1:2] * u1 + w[2:3] * u
    o_ref[...] = (b_ref[...].astype(F32) * y).astype(o_ref.dtype)
    carry_ref[...] = u[rows - 8:rows]


def short_conv(proj, w, *, tr):
    s = proj.shape[0]
    base = (2 * ML_HEADS * ML_QK + 2 * ML_WIDTH) // CV_WIDTH
    return pl.pallas_call(
        _conv_kernel,
        grid=(s // tr,),
        in_specs=[
            pl.BlockSpec((tr, CV_WIDTH), lambda i: (i, base)),
            pl.BlockSpec((tr, CV_WIDTH), lambda i: (i, base + 1)),
            pl.BlockSpec((tr, CV_WIDTH), lambda i: (i, base + 2)),
            pl.BlockSpec((CV_K, CV_WIDTH), lambda i: (0, 0)),
        ],
        out_specs=pl.BlockSpec((tr, CV_WIDTH), lambda i: (i, 0)),
        out_shape=jax.ShapeDtypeStruct((s, CV_WIDTH), BF16),
        scratch_shapes=[pltpu.VMEM((8, CV_WIDTH), F32)],
        compiler_params=_cparams("arbitrary"),
        name="short_conv",
    )(proj, proj, proj, w)


def _da_prep_kernel(q_ref, k_ref, qg_ref, kg_ref, qo_ref, ko_ref):
    r = lax.broadcasted_iota(I32, (LANES, LANES), 0) // DA_QK
    c = lax.broadcasted_iota(I32, (LANES, LANES), 1) // DA_QK
    seg = (r == c).astype(F32)

    def norm(x_ref, g_ref, o_ref, scale):
        for j in range(x_ref.shape[1] // LANES):
            x = x_ref[:, j * LANES:(j + 1) * LANES].astype(F32)
            ss = jnp.dot(x * x, seg, precision=lax.Precision.HIGHEST, preferred_element_type=F32)
            y = x * lax.rsqrt(ss * (1.0 / DA_QK) + EPS) * g_ref[:, j * LANES:(j + 1) * LANES]
            o_ref[:, j * LANES:(j + 1) * LANES] = (y * scale).astype(o_ref.dtype)

    norm(q_ref, qg_ref, qo_ref, DA_QK ** -0.5)
    norm(k_ref, kg_ref, ko_ref, 1.0)


def da_prep(proj, q_g, k_g, *, tr):
    s = proj.shape[0]
    w = DA_HEADS * 2 * DA_QK
    base = (2 * ML_HEADS * ML_QK + 2 * ML_WIDTH + 3 * CV_WIDTH) // w
    qg = jnp.tile(q_g, w // DA_QK).reshape(1, w)
    kg = jnp.tile(k_g, w // DA_QK).reshape(1, w)
    return pl.pallas_call(
        _da_prep_kernel,
        grid=(s // tr,),
        in_specs=[
            pl.BlockSpec((tr, w), lambda i: (i, base)),
            pl.BlockSpec((tr, w), lambda i: (i, base + 1)),
            pl.BlockSpec((1, w), lambda i: (0, 0)),
            pl.BlockSpec((1, w), lambda i: (0, 0)),
        ],
        out_specs=[pl.BlockSpec((tr, w), lambda i: (i, 0)), pl.BlockSpec((tr, w), lambda i: (i, 0))],
        out_shape=[jax.ShapeDtypeStruct((s, w), BF16), jax.ShapeDtypeStruct((s, w), BF16)],
        compiler_params=_cparams("parallel"),
        name="da_prep",
    )(proj, proj, qg, kg)


def _da_kernel(q_ref, k_ref, v_ref, lam_ref, g_ref, o_ref, *, tq, lam_init):
    qi = pl.program_id(1)
    q = q_ref[...]
    lane = lax.broadcasted_iota(I32, q.shape, 1)
    zero = jnp.zeros_like(q)
    q1 = jnp.where(lane < DA_QK, q, zero)
    q2 = jnp.where(lane >= DA_QK, q, zero)

    def scores(kb):
        return (lax.dot_general(q1, kb, _NT, preferred_element_type=F32),
                lax.dot_general(q2, kb, _NT, preferred_element_type=F32))

    def update(carry, s, vb):
        m, l, acc = carry
        m_new = jnp.maximum(m, jnp.max(s, axis=1, keepdims=True))
        alpha = jnp.exp(m - m_new)
        p = jnp.exp(s - m_new)
        l = alpha * l + jnp.sum(p, axis=1, keepdims=True)
        acc = alpha * acc + jnp.dot(p.astype(BF16), vb, preferred_element_type=F32)
        return m_new, l, acc

    def body(ki, carry):
        c1, c2 = carry
        off = pl.multiple_of(ki * tq, tq)
        kb = k_ref[pl.ds(off, tq), :]
        vb = v_ref[pl.ds(off, tq), :]
        s1, s2 = scores(kb)
        return update(c1, s1, vb), update(c2, s2, vb)

    def init():
        return (jnp.full((tq, 1), -jnp.inf, F32), jnp.zeros((tq, 1), F32), jnp.zeros((tq, DA_V), F32))

    c1, c2 = lax.fori_loop(0, qi, body, (init(), init()))
    off = pl.multiple_of(qi * tq, tq)
    kb = k_ref[pl.ds(off, tq), :]
    vb = v_ref[pl.ds(off, tq), :]
    s1, s2 = scores(kb)
    rq = lax.broadcasted_iota(I32, (tq, tq), 0) // CHUNK
    ck = lax.broadcasted_iota(I32, (tq, tq), 1) // CHUNK
    vis = ck <= rq
    _, l1, a1 = update(c1, jnp.where(vis, s1, -jnp.inf), vb)
    _, l2, a2 = update(c2, jnp.where(vis, s2, -jnp.inf), vb)

    lp = lam_ref[...]
    lam = (jnp.exp(jnp.sum(lp[0:1] * lp[1:2], axis=1, keepdims=True))
           - jnp.exp(jnp.sum(lp[2:3] * lp[3:4], axis=1, keepdims=True)) + lam_init)
    o = a1 / l1 - lam * (a2 / l2)
    o_ref[...] = (_rms(o, g_ref[...]) * (1.0 - lam_init)).astype(o_ref.dtype)


def diff_attention(qn, kn, proj, lam_p, subln_g, *, tq, lam_init):
    s = qn.shape[0]
    vbase = (2 * ML_HEADS * ML_QK + 2 * ML_WIDTH + 3 * CV_WIDTH + 2 * DA_HEADS * 2 * DA_QK) // DA_V
    return pl.pallas_call(
        functools.partial(_da_kernel, tq=tq, lam_init=lam_init),
        grid=(DA_HEADS, s // tq),
        in_specs=[
            pl.BlockSpec((tq, 2 * DA_QK), lambda h, i: (i, h)),
            pl.BlockSpec((s, 2 * DA_QK), lambda h, i: (0, h)),
            pl.BlockSpec((s, DA_V), lambda h, i: (0, vbase + h)),
            pl.BlockSpec((4, DA_QK), lambda h, i: (0, 0)),
            pl.BlockSpec((1, DA_V), lambda h, i: (0, 0)),
        ],
        out_specs=pl.BlockSpec((tq, DA_V), lambda h, i: (i, h)),
        out_shape=jax.ShapeDtypeStruct((s, DA_WIDTH), BF16),
        compiler_params=_cparams("parallel", "arbitrary"),
        name="diff_attention",
    )(qn, kn, proj, lam_p, subln_g.reshape(1, DA_V))


def _merge_kernel(h_ref, hm_ref, hc_ref, hd_ref, wg0, wg1, wg2, bg0, bg1, bg2, wbm, wbc, wbd, o_ref):
    h = h_ref[...]

    def gated(wg, bg, hb_ref, wb):
        gate = jax.nn.sigmoid(jnp.dot(h, wg[...], preferred_element_type=F32) + bg[...])
        return gate * jnp.dot(hb_ref[...], wb[...], preferred_element_type=F32)

    out = gated(wg0, bg0, hm_ref, wbm) + gated(wg1, bg1, hc_ref, wbc) + gated(wg2, bg2, hd_ref, wbd)
    o_ref[...] = out.astype(o_ref.dtype)


def gated_merge(h, hm, hc, hd, w_gate, b_gate, w_branch, *, tm, tn):
    s, d = h.shape
    nj = d // tn
    r_c = ML_WIDTH // CV_WIDTH
    r_d = (ML_WIDTH + CV_WIDTH) // DA_WIDTH

    def gspec(b):
        return pl.BlockSpec((d, tn), lambda i, j: (0, j + b * nj))

    def bspec(b):
        return pl.BlockSpec((1, tn), lambda i, j: (0, j + b * nj))

    return pl.pallas_call(
        _merge_kernel,
        grid=(s // tm, nj),
        in_specs=[
            pl.BlockSpec((tm, d), lambda i, j: (i, 0)),
            pl.BlockSpec((tm, ML_WIDTH), lambda i, j: (i, 0)),
            pl.BlockSpec((tm, CV_WIDTH), lambda i, j: (i, 0)),
            pl.BlockSpec((tm, DA_WIDTH), lambda i, j: (i, 0)),
            gspec(0), gspec(1), gspec(2), bspec(0), bspec(1), bspec(2),
            pl.BlockSpec((ML_WIDTH, tn), lambda i, j: (0, j)),
            pl.BlockSpec((CV_WIDTH, tn), lambda i, j: (r_c, j)),
            pl.BlockSpec((DA_WIDTH, tn), lambda i, j: (r_d, j)),
        ],
        out_specs=pl.BlockSpec((tm, tn), lambda i, j: (i, j)),
        out_shape=jax.ShapeDtypeStruct((s, d), BF16),
        compiler_params=_cparams("parallel", "arbitrary"),
        name="gated_merge",
    )(h, hm, hc, hd, w_gate, w_gate, w_gate, b_gate, b_gate, b_gate, w_branch, w_branch, w_branch)


def _xa_kv_kernel(mem_ref, mg_ref, wkv_ref, kg_ref, k_ref, v_ref):
    mn = _rms(mem_ref[...], mg_ref[...]).astype(BF16)
    kv = jnp.dot(mn, wkv_ref[...], preferred_element_type=F32)
    for hh in range(XA_HEADS):
        kh = kv[:, hh * XA_DIM:(hh + 1) * XA_DIM]
        k_ref[:, hh * XA_DIM:(hh + 1) * XA_DIM] = _rms(kh, kg_ref[...]).astype(k_ref.dtype)
    v_ref[...] = kv[:, XA_WIDTH:].astype(v_ref.dtype)


def xa_kv(mem, mem_g, wkv, k_g):
    m, d = mem.shape
    return pl.pallas_call(
        _xa_kv_kernel,
        out_shape=[jax.ShapeDtypeStruct((m, XA_WIDTH), BF16), jax.ShapeDtypeStruct((m, XA_WIDTH), BF16)],
        compiler_params=pltpu.CompilerParams(vmem_limit_bytes=VMEM_LIMIT),
        name="xa_kv",
    )(mem, mem_g.reshape(1, d), wkv, k_g.reshape(1, XA_DIM))


def _xa_kernel(x_ref, ng_ref, wq_ref, qg_ref, k_ref, v_ref, wo_ref, fg_ref, y_ref, yn_ref):
    x = x_ref[...]
    xn = _rms(x, ng_ref[...]).astype(BF16)
    q = jnp.dot(xn, wq_ref[...], preferred_element_type=F32)
    outs = []
    for hh in range(XA_HEADS):
        sl = slice(hh * XA_DIM, (hh + 1) * XA_DIM)
        qh = _rms(q[:, sl], qg_ref[...]).astype(BF16)
        s = lax.dot_general(qh, k_ref[:, sl], _NT, preferred_element_type=F32) * (XA_DIM ** -0.5)
        p = jnp.exp(s - jnp.max(s, axis=1, keepdims=True))
        p = p / jnp.sum(p, axis=1, keepdims=True)
        outs.append(jnp.dot(p.astype(BF16), v_ref[:, sl], preferred_element_type=F32))
    o = jnp.concatenate(outs, axis=1).astype(BF16)
    y = x + jnp.dot(o, wo_ref[...], preferred_element_type=F32)
    y_ref[...] = y
    yn_ref[...] = _rms(y, fg_ref[...]).astype(yn_ref.dtype)


def cross_attention(x, norm_g, wq, q_g, k, v, wo, next_g, *, tm):
    s, d = x.shape
    m = k.shape[0]
    const = lambda i: (0, 0)
    return pl.pallas_call(
        _xa_kernel,
        grid=(s // tm,),
        in_specs=[
            pl.BlockSpec((tm, d), lambda i: (i, 0)),
            pl.BlockSpec((1, d), const),
            pl.BlockSpec((d, XA_WIDTH), const),
            pl.BlockSpec((1, XA_DIM), const),
            pl.BlockSpec((m, XA_WIDTH), const),
            pl.BlockSpec((m, XA_WIDTH), const),
            pl.BlockSpec((XA_WIDTH, d), const),
            pl.BlockSpec((1, d), const),
        ],
        out_specs=[pl.BlockSpec((tm, d), lambda i: (i, 0)), pl.BlockSpec((tm, d), lambda i: (i, 0))],
        out_shape=[jax.ShapeDtypeStruct((s, d), F32), jax.ShapeDtypeStruct((s, d), BF16)],
        compiler_params=_cparams("parallel"),
        name="cross_attention",
    )(x, norm_g.reshape(1, d), wq, q_g.reshape(1, XA_DIM), k, v, wo, next_g.reshape(1, d))


def _top16(s, payload=None):
    n = s.shape[0]
    iota = lax.broadcasted_iota(I32, s.shape, 0)
    vals, sel = [], []
    for _ in range(PEER_TOPK):
        mx = jnp.max(s, axis=0, keepdims=True)
        ix = jnp.min(jnp.where(s == mx, iota, n), axis=0, keepdims=True)
        hit = iota == ix
        vals.append(mx)
        if payload is None:
            sel.append(ix)
        else:
            sel.append(jnp.max(jnp.where(hit, payload, -1), axis=0, keepdims=True))
        s = jnp.where(hit, -jnp.inf, s)
    return jnp.concatenate(vals, axis=0), jnp.concatenate(sel, axis=0)


def _peer_topk_kernel(q_ref, keys_ref, idx_ref, g_ref):
    q = q_ref[...].astype(BF16)
    k1 = keys_ref[0, 0].astype(BF16)
    k2 = keys_ref[0, 1].astype(BF16)
    s1 = lax.dot_general(k1, q[:, :PEER_HALF], _NT, preferred_element_type=F32)
    s2 = lax.dot_general(k2, q[:, PEER_HALF:], _NT, preferred_element_type=F32)
    t1, i1 = _top16(s1)
    t2, i2 = _top16(s2)
    cand = jnp.concatenate([t1[a:a + 1] + t2 for a in range(PEER_TOPK)], axis=0)
    cidx = jnp.concatenate([i1[a:a + 1] * N_KEYS + i2 for a in range(PEER_TOPK)], axis=0)
    top, eidx = _top16(cand, cidx)
    e = jnp.exp(top - top[0:1])
    idx_ref[0] = eidx
    g_ref[0] = e / jnp.sum(e, axis=0, keepdims=True)


def peer_topk(q, keys, *, tt):
    s = q.shape[0]
    return pl.pallas_call(
        _peer_topk_kernel,
        grid=(s // tt, PEER_HEADS),
        in_specs=[
            pl.BlockSpec((tt, 2 * PEER_HALF), lambda i, h: (i, h)),
            pl.BlockSpec((1, 2, N_KEYS, PEER_HALF), lambda i, h: (h, 0, 0, 0)),
        ],
        out_specs=[pl.BlockSpec((1, PEER_TOPK, tt), lambda i, h: (h, 0, i)),
                   pl.BlockSpec((1, PEER_TOPK, tt), lambda i, h: (h, 0, i))],
        out_shape=[jax.ShapeDtypeStruct((PEER_HEADS, PEER_TOPK, s), I32),
                   jax.ShapeDtypeStruct((PEER_HEADS, PEER_TOPK, s), F32)],
        compiler_params=_cparams("parallel", "parallel"),
        name="peer_topk",
    )(q, keys)


PEER_NBUF = 8
PEER_AHEAD = 5
PEER_GROUP = 8
PEER_UROWS = 16
PEER_ROWS = 2 * PEER_UROWS
PEER_PITCH = 40
HI_MASK = 0xFFFF0000


def _peer_mix_kernel(idx_ref, g_ref, x_ref, ng_ref, tab_ref, o_ref, idx_smem, xn_ref, xb_ref, ps_ref, buf, sems, idx_sem, *, tb, half):
    cp = pltpu.make_async_copy(idx_ref, idx_smem, idx_sem)
    cp.start()
    xn_ref[...] = _rms(x_ref[...], ng_ref[...])
    cp.wait()

    def issue_rows(t, slot, e0, e1):
        for e in range(e0, e1):
            src = tab_ref.at[pl.ds(pl.multiple_of(idx_smem[t, e], PEER_ROWS), PEER_ROWS), :]
            dst = buf.at[slot, pl.ds(e * PEER_PITCH, PEER_ROWS), :]
            pltpu.make_async_copy(src, dst, sems.at[slot]).start(priority=e % 2)

    def wait(slot):
        nrow = PEER_SEL * PEER_ROWS
        pltpu.make_async_copy(tab_ref.at[pl.ds(0, nrow), :], buf.at[slot, pl.ds(0, nrow), :], sems.at[slot]).wait()

    ones8 = jnp.ones((8, LANES), BF16)
    er = lax.broadcasted_iota(I32, (PEER_SEL, 2 * PEER_SEL), 0)
    ec = lax.broadcasted_iota(I32, (PEER_SEL, 2 * PEER_SEL), 1)
    expand = (ec // 2 == er).astype(BF16)
    lane = lax.broadcasted_iota(I32, (8, 2 * PEER_SEL), 1)
    sub = lax.broadcasted_iota(I32, (8, 2 * PEER_SEL), 0)
    pick = sub == (lane % 2)
    ngrp = PEER_SEL // PEER_GROUP

    def rows(slot, grp, r):
        return buf[slot, pl.ds(grp * PEER_GROUP * PEER_PITCH + r, PEER_GROUP, stride=PEER_PITCH), :]

    def key_group(slot, grp):
        acc = None
        for r in range(PEER_UROWS):
            w = rows(slot, grp, r)
            lo = pltpu.bitcast(w << 16, F32)
            hi = pltpu.bitcast(w & jnp.uint32(HI_MASK), F32)
            term = lo * xb_ref[:, r * LANES:(r + 1) * LANES] + hi * xb_ref[:, half + r * LANES:half + (r + 1) * LANES]
            acc = term if acc is None else acc + term
        return acc

    def key_scores(par):
        psum = ps_ref[par]
        p_hi = psum.astype(BF16)
        p_lo = (psum - p_hi.astype(F32)).astype(BF16)
        return (lax.dot_general(ones8, p_hi, _NT, preferred_element_type=F32)
                + lax.dot_general(ones8, p_lo, _NT, preferred_element_type=F32))

    def gate_weights(t, act):
        wgt = jax.nn.gelu(act) * g_ref[pl.ds(t, 1), :]
        wexp = jnp.dot(wgt.astype(BF16), expand, preferred_element_type=F32)
        return jnp.where(pick, wexp, 0.0).astype(BF16)

    def value_dot(slot, r, w8):
        vw = jnp.concatenate([rows(slot, grp, PEER_UROWS + r) for grp in range(ngrp)], axis=0)
        o = jnp.dot(w8, pltpu.bitcast(vw, BF16), preferred_element_type=F32)
        return o[0:1], o[1:2]

    def finish(t, lo, hi):
        o_ref[pl.ds(t, 1), :] = x_ref[pl.ds(t, 1), :] + jnp.concatenate(lo + hi, axis=1)

    def region(t, k, w8_prev):
        wait(k)
        xb_ref[...] = jnp.broadcast_to(xn_ref[pl.ds(t, 1), :], xb_ref.shape)
        t_gate = jnp.maximum(t - 1, 0)
        t_mix = jnp.maximum(t - 2, 0)
        parts, lo, hi = [], [], []
        for part in range(2):
            if part == 0:
                act = key_scores((k - 1) % 2)
            else:
                w8 = gate_weights(t_gate, act)
                w8 = jnp.where(t >= 1, w8, jnp.zeros_like(w8))
            for r in range(part * PEER_UROWS // 2, (part + 1) * PEER_UROWS // 2):
                a, b = value_dot((k - 2) % PEER_NBUF, r, w8_prev)
                lo.append(a)
                hi.append(b)
            for grp in range(part * ngrp // 2, (part + 1) * ngrp // 2):
                parts.append(key_group(k, grp))
            e0, e1 = part * PEER_SEL // 2, (part + 1) * PEER_SEL // 2
            if k + PEER_AHEAD < PEER_NBUF:
                issue_rows(t + PEER_AHEAD, (k + PEER_AHEAD) % PEER_NBUF, e0, e1)
            else:
                @pl.when(t + PEER_AHEAD < tb)
                def _():
                    issue_rows(t + PEER_AHEAD, (k + PEER_AHEAD) % PEER_NBUF, e0, e1)
        ps_ref[k % 2] = jnp.concatenate(parts, axis=0)
        finish(t_mix, lo, hi)
        return w8

    @pl.when(pl.program_id(0) == 0)
    def _():
        ps_ref[...] = jnp.zeros_like(ps_ref)
        for k in range(PEER_NBUF - 2, PEER_NBUF):
            buf[k] = jnp.zeros(buf.shape[1:], U32)

    def prime(k, _):
        issue_rows(k, k, 0, PEER_SEL)
        return 0

    lax.fori_loop(0, PEER_AHEAD, prime, 0)

    def body(i, w8):
        for k in range(PEER_NBUF):
            w8 = region(i * PEER_NBUF + k, k, w8)
        return w8

    w8 = lax.fori_loop(0, tb // PEER_NBUF, body, jnp.zeros((8, 2 * PEER_SEL), BF16))
    lo, hi = zip(*[value_dot(PEER_NBUF - 2, r, w8) for r in range(PEER_UROWS)])
    finish(tb - 2, list(lo), list(hi))
    w8 = gate_weights(tb - 1, key_scores((PEER_NBUF - 1) % 2))
    lo, hi = zip(*[value_dot(PEER_NBUF - 1, r, w8) for r in range(PEER_UROWS)])
    finish(tb - 1, list(lo), list(hi))


def peer_mix(eidx, g, x, norm_g, table, *, tb):
    s, d = x.shape
    half = d // 2
    assert half == PEER_UROWS * LANES and tb % PEER_NBUF == 0 and tb // PEER_NBUF >= 2
    return pl.pallas_call(
        functools.partial(_peer_mix_kernel, tb=tb, half=half),
        grid=(s // tb,),
        in_specs=[
            pl.BlockSpec((tb, PEER_SEL), lambda i: (i, 0)),
            pl.BlockSpec((tb, PEER_SEL), lambda i: (i, 0)),
            pl.BlockSpec((tb, d), lambda i: (i, 0)),
            pl.BlockSpec((1, d), lambda i: (0, 0)),
            pl.BlockSpec(memory_space=pl.ANY),
        ],
        out_specs=pl.BlockSpec((tb, d), lambda i: (i, 0)),
        out_shape=jax.ShapeDtypeStruct((s, d), F32),
        scratch_shapes=[
            pltpu.SMEM((tb, PEER_SEL), I32),
            pltpu.VMEM((tb, d), F32),
            pltpu.VMEM((8, d), F32),
            pltpu.VMEM((2, PEER_SEL, LANES), F32),
            pltpu.VMEM((PEER_NBUF, PEER_SEL * PEER_PITCH, LANES), U32),
            pltpu.SemaphoreType.DMA((PEER_NBUF,)),
            pltpu.SemaphoreType.DMA,
        ],
        compiler_params=_cparams("arbitrary"),
        name="peer_mix",
    )(eidx, g, x, norm_g.reshape(1, d), table)


def _pack_expert_table(u, v):
    def pack(w):
        half = w.shape[1] // 2
        b = lax.bitcast_convert_type(w.astype(BF16), jnp.uint16).astype(U32)
        return b[:, :half] | (b[:, half:] << 16)
    return jnp.concatenate([pack(u), pack(v)], axis=1).reshape(-1, LANES)


def _layer(x, h, mem, lam_init, p):
    s, d = x.shape
    n_qkvo = 2 * ML_HEADS * ML_QK + 2 * ML_WIDTH
    proj = matmul(h, p["w_main"], tm=1024, tn=1024, out_dtype=BF16, name="in_proj")
    gates_if = matmul(h, p["w_if"], tm=1024, tn=LANES, out_dtype=F32, bias=p["b_if"], name="if_proj")
    kt = proj[:, ML_HEADS * ML_QK:2 * ML_HEADS * ML_QK].T
    if_row = gates_if[:, :2 * ML_HEADS].T
    hm = mlstm(proj, kt, gates_if, if_row, p["ml_norm"], L=128)
    hc = short_conv(proj, p["conv_w"], tr=512)
    qn, kn = da_prep(proj, p["da_q_norm"], p["da_k_norm"], tr=512)
    hd = diff_attention(qn, kn, proj, p["da_lambda"], p["da_subln"], tq=512, lam_init=lam_init)
    merged = gated_merge(h, hm, hc, hd, p["w_gate"], p["b_gate"], p["w_branch"], tm=512, tn=256)
    x = matmul(merged, p["w_out"], tm=512, tn=1024, out_dtype=F32, residual=x, name="out_proj")
    k, v = xa_kv(mem, p["mem_norm"], p["xa_wkv"], p["xa_k_norm"])
    x, xn = cross_attention(x, p["xa_norm"], p["xa_wq"], p["xa_q_norm"], k, v, p["xa_wo"], p["ffn_norm"], tm=256)
    q = matmul(xn, p["peer_wq"], tm=1024, tn=1024, out_dtype=F32, name="peer_q")
    eidx, g = peer_topk(q, p["peer_keys"], tt=512)
    eidx = jnp.transpose(eidx, (2, 0, 1)).reshape(s, PEER_SEL) * PEER_ROWS
    g = jnp.transpose(g, (2, 0, 1)).reshape(s, PEER_SEL)
    return peer_mix(eidx, g, x, p["ffn_norm"], p["table"], tb=128)


def kernel(x, mem, mix_norm, w_in, b_if, ml_norm, conv_w, da_q_norm, da_k_norm, da_lambda, da_subln, w_gate, b_gate, w_branch, w_out, xa_norm, mem_norm, xa_wq, xa_wkv, xa_q_norm, xa_k_norm, xa_wo, ffn_norm, peer_wq, peer_keys, peer_u, peer_v):
    depth = w_in.shape[0]
    d = x.shape[-1]
    xs = x.reshape(-1, d)
    mems = mem.reshape(-1, d)
    n_qkvo = 2 * ML_HEADS * ML_QK + 2 * ML_WIDTH
    for l in range(depth):
        wl = w_in[l]
        w_if = jnp.pad(wl[:, n_qkvo:n_qkvo + 2 * ML_HEADS], ((0, 0), (0, LANES - 2 * ML_HEADS)))
        bias_if = jnp.pad(b_if[l].reshape(1, 2 * ML_HEADS), ((0, 0), (0, LANES - 2 * ML_HEADS)))
        p = dict(
            w_main=jnp.concatenate([wl[:, :n_qkvo], wl[:, n_qkvo + 2 * ML_HEADS:]], axis=1).astype(BF16),
            w_if=w_if.astype(BF16), b_if=bias_if,
            ml_norm=ml_norm[l], conv_w=conv_w[l], da_q_norm=da_q_norm[l], da_k_norm=da_k_norm[l],
            da_lambda=da_lambda[l], da_subln=da_subln[l],
            w_gate=w_gate[l].astype(BF16), b_gate=b_gate[l].reshape(1, -1), w_branch=w_branch[l].astype(BF16),
            w_out=w_out[l].astype(BF16), xa_norm=xa_norm[l], mem_norm=mem_norm[l], xa_wq=xa_wq[l].astype(BF16),
            xa_wkv=xa_wkv[l].astype(BF16), xa_q_norm=xa_q_norm[l], xa_k_norm=xa_k_norm[l],
            xa_wo=xa_wo[l].astype(BF16), ffn_norm=ffn_norm[l], peer_wq=peer_wq[l].astype(BF16),
            peer_keys=peer_keys[l], table=_pack_expert_table(peer_u[l], peer_v[l]),
        )
        h = rmsnorm(xs, mix_norm[l], tr=256)
        lam_init = 0.8 - 0.6 * math.exp(-0.3 * l)
        xs = _layer(xs, h, mems, lam_init, p)
    return xs.reshape(x.shape)
```

```python
import functools
import math

import jax
import jax.numpy as jnp
import numpy as np
from jax import lax
from jax.experimental import pallas as pl
from jax.experimental.pallas import tpu as pltpu

F32 = jnp.float32
BF16 = jnp.bfloat16
I32 = jnp.int32
U32 = jnp.uint32

EPS = 1e-6
CHUNK = 64

ML_HEADS, ML_QK, ML_V = 8, 128, 256
ML_WIDTH = ML_HEADS * ML_V
CV_WIDTH, CV_K = 1024, 3
DA_HEADS, DA_QK, DA_V = 8, 64, 128
DA_WIDTH = DA_HEADS * DA_V
XA_HEADS, XA_DIM = 4, 128
XA_WIDTH = XA_HEADS * XA_DIM
PEER_HEADS, N_KEYS, PEER_TOPK = 8, 128, 16
PEER_HALF = 128
PEER_SEL = PEER_HEADS * PEER_TOPK

V7X_VMEM_BYTES = 64 * 1024 * 1024
VMEM_LIMIT = V7X_VMEM_BYTES - 8 * 1024 * 1024
LANES = 128

_NT = (((1,), (1,)), ((), ()))


def _cparams(*sem):
    return pltpu.CompilerParams(dimension_semantics=sem, vmem_limit_bytes=VMEM_LIMIT)


def _rms(x, g):
    return x * lax.rsqrt(jnp.mean(x * x, axis=-1, keepdims=True) + EPS) * g


def _rmsnorm_kernel(x_ref, g_ref, o_ref):
    o_ref[...] = _rms(x_ref[...].astype(F32), g_ref[...]).astype(o_ref.dtype)


def rmsnorm(x, g, *, tr, out_dtype=BF16):
    rows, width = x.shape
    return pl.pallas_call(
        _rmsnorm_kernel,
        grid=(rows // tr,),
        in_specs=[pl.BlockSpec((tr, width), lambda i: (i, 0)), pl.BlockSpec((1, width), lambda i: (0, 0))],
        out_specs=pl.BlockSpec((tr, width), lambda i: (i, 0)),
        out_shape=jax.ShapeDtypeStruct((rows, width), out_dtype),
        compiler_params=_cparams("parallel"),
        name="rmsnorm",
    )(x, g.reshape(1, width))


def _mm_kernel(a_ref, b_ref, o_ref):
    o_ref[...] = jnp.dot(a_ref[...], b_ref[...], preferred_element_type=F32).astype(o_ref.dtype)


def _mm_bias_kernel(a_ref, b_ref, bias_ref, o_ref):
    acc = jnp.dot(a_ref[...], b_ref[...], preferred_element_type=F32)
    o_ref[...] = (acc + bias_ref[...]).astype(o_ref.dtype)


def _mm_res_kernel(a_ref, b_ref, r_ref, o_ref):
    acc = jnp.dot(a_ref[...], b_ref[...], preferred_element_type=F32)
    o_ref[...] = (r_ref[...] + acc).astype(o_ref.dtype)


def matmul(a, b, *, tm, tn, out_dtype, bias=None, residual=None, name="matmul"):
    m, k = a.shape
    n = b.shape[1]
    in_specs = [pl.BlockSpec((tm, k), lambda i, j: (i, 0)), pl.BlockSpec((k, tn), lambda i, j: (0, j))]
    args = [a, b]
    kern = _mm_kernel
    if bias is not None:
        kern = _mm_bias_kernel
        in_specs.append(pl.BlockSpec((1, tn), lambda i, j: (0, j)))
        args.append(bias)
    if residual is not None:
        kern = _mm_res_kernel
        in_specs.append(pl.BlockSpec((tm, tn), lambda i, j: (i, j)))
        args.append(residual)
    return pl.pallas_call(
        kern,
        grid=(m // tm, n // tn),
        in_specs=in_specs,
        out_specs=pl.BlockSpec((tm, tn), lambda i, j: (i, j)),
        out_shape=jax.ShapeDtypeStruct((m, n), out_dtype),
        compiler_params=_cparams("parallel", "arbitrary"),
        name=name,
    )(*args)


def _log_sigmoid(x):
    return jnp.minimum(x, 0.0) - jnp.log(1.0 + jnp.exp(-jnp.abs(x)))


def _mlstm_kernel(q_ref, kt_ref, k_ref, v_ref, o_ref, ifc_ref, ifr_ref, g_ref, out_ref, ct_ref, n_ref, m_ref, *, L):
    c = pl.program_id(0)

    @pl.when(c == 0)
    def _():
        ct_ref[...] = jnp.zeros_like(ct_ref)
        n_ref[...] = jnp.zeros_like(n_ref)
        m_ref[...] = jnp.zeros_like(m_ref)

    row = lax.broadcasted_iota(I32, (L, L), 0)
    col = lax.broadcasted_iota(I32, (L, L), 1)
    tril = col <= row
    ltri = tril.astype(F32)
    utri = (row <= col).astype(F32)
    ifc = ifc_ref[...]
    ifr = ifr_ref[...]
    b_cols = jnp.dot(ltri, _log_sigmoid(ifc), precision=lax.Precision.HIGHEST, preferred_element_type=F32)
    b_rows = jnp.dot(_log_sigmoid(ifr), utri, precision=lax.Precision.HIGHEST, preferred_element_type=F32)
    scale = ML_QK ** -0.5
    for h in range(ML_HEADS):
        q = q_ref[:, h * ML_QK:(h + 1) * ML_QK]
        kt = kt_ref[h * ML_QK:(h + 1) * ML_QK, :]
        k = k_ref[:, h * ML_QK:(h + 1) * ML_QK]
        v = v_ref[:, h * ML_V:(h + 1) * ML_V]
        bc = b_cols[:, ML_HEADS + h:ML_HEADS + h + 1]
        br = b_rows[ML_HEADS + h:ML_HEADS + h + 1, :]
        ic = ifc[:, h:h + 1]
        ir = ifr[h:h + 1, :]
        m_prev = m_ref[h:h + 1, 0:1]
        n_prev = n_ref[h:h + 1, :]
        ct_prev = ct_ref[h]

        d = jnp.where(tril, bc - br + ir, -jnp.inf)
        inter = bc + m_prev
        mt = jnp.maximum(inter, jnp.max(d, axis=1, keepdims=True))
        qk = jnp.dot(q, kt, preferred_element_type=F32) * scale
        w = jnp.exp(d - mt) * qk
        s_prev = jnp.exp(inter - mt)
        num = jnp.dot(w.astype(BF16), v, preferred_element_type=F32)
        num = num + s_prev * (jnp.dot(q, ct_prev.astype(BF16), preferred_element_type=F32) * scale)
        qn = jnp.sum(q.astype(F32) * n_prev, axis=1, keepdims=True) * scale
        den = jnp.sum(w, axis=1, keepdims=True) + s_prev * qn
        hh = num / jnp.maximum(jnp.abs(den), jnp.exp(-mt))

        bl = bc[L - 1:L, :]
        a_col = bl - bc + ic
        a_row = bl - br + ir
        m_new = jnp.maximum(bl + m_prev, jnp.max(a_row, axis=1, keepdims=True))
        decay = jnp.exp(bl + m_prev - m_new)
        wa = jnp.exp(a_col - m_new)
        vf = v.astype(F32)
        ct_ref[h] = decay * ct_prev + jnp.dot(kt, (wa * vf).astype(BF16), preferred_element_type=F32)
        n_ref[h:h + 1, :] = decay * n_prev + jnp.sum(wa * k.astype(F32), axis=0, keepdims=True)
        m_ref[h:h + 1, :] = jnp.broadcast_to(m_new, (1, LANES))

        gain = g_ref[:, h * ML_V:(h + 1) * ML_V]
        og = o_ref[:, h * ML_V:(h + 1) * ML_V].astype(F32)
        out_ref[:, h * ML_V:(h + 1) * ML_V] = (_rms(hh, gain) * jax.nn.sigmoid(og)).astype(out_ref.dtype)


def mlstm(proj, kt, if_col, if_row, gain, *, L):
    s = proj.shape[0]
    qk_w = ML_HEADS * ML_QK
    return pl.pallas_call(
        functools.partial(_mlstm_kernel, L=L),
        grid=(s // L,),
        in_specs=[
            pl.BlockSpec((L, qk_w), lambda c: (c, 0)),
            pl.BlockSpec((qk_w, L), lambda c: (0, c)),
            pl.BlockSpec((L, qk_w), lambda c: (c, 1)),
            pl.BlockSpec((L, ML_WIDTH), lambda c: (c, 1)),
            pl.BlockSpec((L, ML_WIDTH), lambda c: (c, 2)),
            pl.BlockSpec((L, LANES), lambda c: (c, 0)),
            pl.BlockSpec((2 * ML_HEADS, L), lambda c: (0, c)),
            pl.BlockSpec((1, ML_WIDTH), lambda c: (0, 0)),
        ],
        out_specs=pl.BlockSpec((L, ML_WIDTH), lambda c: (c, 0)),
        out_shape=jax.ShapeDtypeStruct((s, ML_WIDTH), BF16),
        scratch_shapes=[
            pltpu.VMEM((ML_HEADS, ML_QK, ML_V), F32),
            pltpu.VMEM((ML_HEADS, ML_QK), F32),
            pltpu.VMEM((ML_HEADS, LANES), F32),
        ],
        compiler_params=_cparams("arbitrary"),
        name="mlstm",
    )(proj, kt, proj, proj, proj, if_col, if_row, gain.reshape(1, ML_WIDTH))


def _conv_kernel(b_ref, c_ref, h_ref, w_ref, o_ref, carry_ref):
    @pl.when(pl.program_id(0) == 0)
    def _():
        carry_ref[...] = jnp.zeros_like(carry_ref)

    u = c_ref[...].astype(F32) * h_ref[...].astype(F32)
    rows = u.shape[0]
    ext = jnp.concatenate([carry_ref[...], u], axis=0)
    u1 = ext[7:7 + rows]
    u2 = ext[6:6 + rows]
    w = w_ref[...]
    y = w[0:1] * u2 + w[1:2] * u1 + w[2:3] * u
    o_ref[...] = (b_ref[...].astype(F32) * y).astype(o_ref.dtype)
    carry_ref[...] = u[rows - 8:rows]


def short_conv(proj, w, *, tr):
    s = proj.shape[0]
    base = (2 * ML_HEADS * ML_QK + 2 * ML_WIDTH) // CV_WIDTH
    return pl.pallas_call(
        _conv_kernel,
        grid=(s // tr,),
        in_specs=[
            pl.BlockSpec((tr, CV_WIDTH), lambda i: (i, base)),
            pl.BlockSpec((tr, CV_WIDTH), lambda i: (i, base + 1)),
            pl.BlockSpec((tr, CV_WIDTH), lambda i: (i, base + 2)),
            pl.BlockSpec((CV_K, CV_WIDTH), lambda i: (0, 0)),
        ],
        out_specs=pl.BlockSpec((tr, CV_WIDTH), lambda i: (i, 0)),
        out_shape=jax.ShapeDtypeStruct((s, CV_WIDTH), BF16),
        scratch_shapes=[pltpu.VMEM((8, CV_WIDTH), F32)],
        compiler_params=_cparams("arbitrary"),
        name="short_conv",
    )(proj, proj, proj, w)


def _da_prep_kernel(q_ref, k_ref, qg_ref, kg_ref, qo_ref, ko_ref):
    r = lax.broadcasted_iota(I32, (LANES, LANES), 0) // DA_QK
    c = lax.broadcasted_iota(I32, (LANES, LANES), 1) // DA_QK
    seg = (r == c).astype(F32)

    def norm(x_ref, g_ref, o_ref, scale):
        for j in range(x_ref.shape[1] // LANES):
            x = x_ref[:, j * LANES:(j + 1) * LANES].astype(F32)
            ss = jnp.dot(x * x, seg, precision=lax.Precision.HIGHEST, preferred_element_type=F32)
            y = x * lax.rsqrt(ss * (1.0 / DA_QK) + EPS) * g_ref[:, j * LANES:(j + 1) * LANES]
            o_ref[:, j * LANES:(j + 1) * LANES] = (y * scale).astype(o_ref.dtype)

    norm(q_ref, qg_ref, qo_ref, DA_QK ** -0.5)
    norm(k_ref, kg_ref, ko_ref, 1.0)


def da_prep(proj, q_g, k_g, *, tr):
    s = proj.shape[0]
    w = DA_HEADS * 2 * DA_QK
    base = (2 * ML_HEADS * ML_QK + 2 * ML_WIDTH + 3 * CV_WIDTH) // w
    qg = jnp.tile(q_g, w // DA_QK).reshape(1, w)
    kg = jnp.tile(k_g, w // DA_QK).reshape(1, w)
    return pl.pallas_call(
        _da_prep_kernel,
        grid=(s // tr,),
        in_specs=[
            pl.BlockSpec((tr, w), lambda i: (i, base)),
            pl.BlockSpec((tr, w), lambda i: (i, base + 1)),
            pl.BlockSpec((1, w), lambda i: (0, 0)),
            pl.BlockSpec((1, w), lambda i: (0, 0)),
        ],
        out_specs=[pl.BlockSpec((tr, w), lambda i: (i, 0)), pl.BlockSpec((tr, w), lambda i: (i, 0))],
        out_shape=[jax.ShapeDtypeStruct((s, w), BF16), jax.ShapeDtypeStruct((s, w), BF16)],
        compiler_params=_cparams("parallel"),
        name="da_prep",
    )(proj, proj, qg, kg)


def _da_kernel(q_ref, k_ref, v_ref, lam_ref, g_ref, o_ref, *, tq, lam_init):
    qi = pl.program_id(1)
    q = q_ref[...]
    lane = lax.broadcasted_iota(I32, q.shape, 1)
    zero = jnp.zeros_like(q)
    q1 = jnp.where(lane < DA_QK, q, zero)
    q2 = jnp.where(lane >= DA_QK, q, zero)

    def scores(kb):
        return (lax.dot_general(q1, kb, _NT, preferred_element_type=F32),
                lax.dot_general(q2, kb, _NT, preferred_element_type=F32))

    def update(carry, s, vb):
        m, l, acc = carry
        m_new = jnp.maximum(m, jnp.max(s, axis=1, keepdims=True))
        alpha = jnp.exp(m - m_new)
        p = jnp.exp(s - m_new)
        l = alpha * l + jnp.sum(p, axis=1, keepdims=True)
        acc = alpha * acc + jnp.dot(p.astype(BF16), vb, preferred_element_type=F32)
        return m_new, l, acc

    def body(ki, carry):
        c1, c2 = carry
        off = pl.multiple_of(ki * tq, tq)
        kb = k_ref[pl.ds(off, tq), :]
        vb = v_ref[pl.ds(off, tq), :]
        s1, s2 = scores(kb)
        return update(c1, s1, vb), update(c2, s2, vb)

    def init():
        return (jnp.full((tq, 1), -jnp.inf, F32), jnp.zeros((tq, 1), F32), jnp.zeros((tq, DA_V), F32))

    c1, c2 = lax.fori_loop(0, qi, body, (init(), init()))
    off = pl.multiple_of(qi * tq, tq)
    kb = k_ref[pl.ds(off, tq), :]
    vb = v_ref[pl.ds(off, tq), :]
    s1, s2 = scores(kb)
    rq = lax.broadcasted_iota(I32, (tq, tq), 0) // CHUNK
    ck = lax.broadcasted_iota(I32, (tq, tq), 1) // CHUNK
    vis = ck <= rq
    _, l1, a1 = update(c1, jnp.where(vis, s1, -jnp.inf), vb)
    _, l2, a2 = update(c2, jnp.where(vis, s2, -jnp.inf), vb)

    lp = lam_ref[...]
    lam = (jnp.exp(jnp.sum(lp[0:1] * lp[1:2], axis=1, keepdims=True))
           - jnp.exp(jnp.sum(lp[2:3] * lp[3:4], axis=1, keepdims=True)) + lam_init)
    o = a1 / l1 - lam * (a2 / l2)
    o_ref[...] = (_rms(o, g_ref[...]) * (1.0 - lam_init)).astype(o_ref.dtype)


def diff_attention(qn, kn, proj, lam_p, subln_g, *, tq, lam_init):
    s = qn.shape[0]
    vbase = (2 * ML_HEADS * ML_QK + 2 * ML_WIDTH + 3 * CV_WIDTH + 2 * DA_HEADS * 2 * DA_QK) // DA_V
    return pl.pallas_call(
        functools.partial(_da_kernel, tq=tq, lam_init=lam_init),
        grid=(DA_HEADS, s // tq),
        in_specs=[
            pl.BlockSpec((tq, 2 * DA_QK), lambda h, i: (i, h)),
            pl.BlockSpec((s, 2 * DA_QK), lambda h, i: (0, h)),
            pl.BlockSpec((s, DA_V), lambda h, i: (0, vbase + h)),
            pl.BlockSpec((4, DA_QK), lambda h, i: (0, 0)),
            pl.BlockSpec((1, DA_V), lambda h, i: (0, 0)),
        ],
        out_specs=pl.BlockSpec((tq, DA_V), lambda h, i: (i, h)),
        out_shape=jax.ShapeDtypeStruct((s, DA_WIDTH), BF16),
        compiler_params=_cparams("parallel", "arbitrary"),
        name="diff_attention",
    )(qn, kn, proj, lam_p, subln_g.reshape(1, DA_V))


def _merge_kernel(h_ref, hm_ref, hc_ref, hd_ref, wg0, wg1, wg2, bg0, bg1, bg2, wbm, wbc, wbd, o_ref):
    h = h_ref[...]

    def gated(wg, bg, hb_ref, wb):
        gate = jax.nn.sigmoid(jnp.dot(h, wg[...], preferred_element_type=F32) + bg[...])
        return gate * jnp.dot(hb_ref[...], wb[...], preferred_element_type=F32)

    out = gated(wg0, bg0, hm_ref, wbm) + gated(wg1, bg1, hc_ref, wbc) + gated(wg2, bg2, hd_ref, wbd)
    o_ref[...] = out.astype(o_ref.dtype)


def gated_merge(h, hm, hc, hd, w_gate, b_gate, w_branch, *, tm, tn):
    s, d = h.shape
    nj = d // tn
    r_c = ML_WIDTH // CV_WIDTH
    r_d = (ML_WIDTH + CV_WIDTH) // DA_WIDTH

    def gspec(b):
        return pl.BlockSpec((d, tn), lambda i, j: (0, j + b * nj))

    def bspec(b):
        return pl.BlockSpec((1, tn), lambda i, j: (0, j + b * nj))

    return pl.pallas_call(
        _merge_kernel,
        grid=(s // tm, nj),
        in_specs=[
            pl.BlockSpec((tm, d), lambda i, j: (i, 0)),
            pl.BlockSpec((tm, ML_WIDTH), lambda i, j: (i, 0)),
            pl.BlockSpec((tm, CV_WIDTH), lambda i, j: (i, 0)),
            pl.BlockSpec((tm, DA_WIDTH), lambda i, j: (i, 0)),
            gspec(0), gspec(1), gspec(2), bspec(0), bspec(1), bspec(2),
            pl.BlockSpec((ML_WIDTH, tn), lambda i, j: (0, j)),
            pl.BlockSpec((CV_WIDTH, tn), lambda i, j: (r_c, j)),
            pl.BlockSpec((DA_WIDTH, tn), lambda i, j: (r_d, j)),
        ],
        out_specs=pl.BlockSpec((tm, tn), lambda i, j: (i, j)),
        out_shape=jax.ShapeDtypeStruct((s, d), BF16),
        compiler_params=_cparams("parallel", "arbitrary"),
        name="gated_merge",
    )(h, hm, hc, hd, w_gate, w_gate, w_gate, b_gate, b_gate, b_gate, w_branch, w_branch, w_branch)


def _xa_kv_kernel(mem_ref, mg_ref, wkv_ref, kg_ref, k_ref, v_ref):
    mn = _rms(mem_ref[...], mg_ref[...]).astype(BF16)
    kv = jnp.dot(mn, wkv_ref[...], preferred_element_type=F32)
    for hh in range(XA_HEADS):
        kh = kv[:, hh * XA_DIM:(hh + 1) * XA_DIM]
        k_ref[:, hh * XA_DIM:(hh + 1) * XA_DIM] = _rms(kh, kg_ref[...]).astype(k_ref.dtype)
    v_ref[...] = kv[:, XA_WIDTH:].astype(v_ref.dtype)


def xa_kv(mem, mem_g, wkv, k_g):
    m, d = mem.shape
    return pl.pallas_call(
        _xa_kv_kernel,
        out_shape=[jax.ShapeDtypeStruct((m, XA_WIDTH), BF16), jax.ShapeDtypeStruct((m, XA_WIDTH), BF16)],
        compiler_params=pltpu.CompilerParams(vmem_limit_bytes=VMEM_LIMIT),
        name="xa_kv",
    )(mem, mem_g.reshape(1, d), wkv, k_g.reshape(1, XA_DIM))


def _xa_kernel(x_ref, ng_ref, wq_ref, qg_ref, k_ref, v_ref, wo_ref, fg_ref, y_ref, yn_ref):
    x = x_ref[...]
    xn = _rms(x, ng_ref[...]).astype(BF16)
    q = jnp.dot(xn, wq_ref[...], preferred_element_type=F32)
    outs = []
    for hh in range(XA_HEADS):
        sl = slice(hh * XA_DIM, (hh + 1) * XA_DIM)
        qh = _rms(q[:, sl], qg_ref[...]).astype(BF16)
        s = lax.dot_general(qh, k_ref[:, sl], _NT, preferred_element_type=F32) * (XA_DIM ** -0.5)
        p = jnp.exp(s - jnp.max(s, axis=1, keepdims=True))
        p = p / jnp.sum(p, axis=1, keepdims=True)
        outs.append(jnp.dot(p.astype(BF16), v_ref[:, sl], preferred_element_type=F32))
    o = jnp.concatenate(outs, axis=1).astype(BF16)
    y = x + jnp.dot(o, wo_ref[...], preferred_element_type=F32)
    y_ref[...] = y
    yn_ref[...] = _rms(y, fg_ref[...]).astype(yn_ref.dtype)


def cross_attention(x, norm_g, wq, q_g, k, v, wo, next_g, *, tm):
    s, d = x.shape
    m = k.shape[0]
    const = lambda i: (0, 0)
    return pl.pallas_call(
        _xa_kernel,
        grid=(s // tm,),
        in_specs=[
            pl.BlockSpec((tm, d), lambda i: (i, 0)),
            pl.BlockSpec((1, d), const),
            pl.BlockSpec((d, XA_WIDTH), const),
            pl.BlockSpec((1, XA_DIM), const),
            pl.BlockSpec((m, XA_WIDTH), const),
            pl.BlockSpec((m, XA_WIDTH), const),
            pl.BlockSpec((XA_WIDTH, d), const),
            pl.BlockSpec((1, d), const),
        ],
        out_specs=[pl.BlockSpec((tm, d), lambda i: (i, 0)), pl.BlockSpec((tm, d), lambda i: (i, 0))],
        out_shape=[jax.ShapeDtypeStruct((s, d), F32), jax.ShapeDtypeStruct((s, d), BF16)],
        compiler_params=_cparams("parallel"),
        name="cross_attention",
    )(x, norm_g.reshape(1, d), wq, q_g.reshape(1, XA_DIM), k, v, wo, next_g.reshape(1, d))


def _top16(s, payload=None):
    n = s.shape[0]
    iota = lax.broadcasted_iota(I32, s.shape, 0)
    vals, sel = [], []
    for _ in range(PEER_TOPK):
        mx = jnp.max(s, axis=0, keepdims=True)
        ix = jnp.min(jnp.where(s == mx, iota, n), axis=0, keepdims=True)
        hit = iota == ix
        vals.append(mx)
        if payload is None:
            sel.append(ix)
        else:
            sel.append(jnp.max(jnp.where(hit, payload, -1), axis=0, keepdims=True))
        s = jnp.where(hit, -jnp.inf, s)
    return jnp.concatenate(vals, axis=0), jnp.concatenate(sel, axis=0)


def _peer_topk_kernel(q_ref, keys_ref, idx_ref, g_ref):
    q = q_ref[...].astype(BF16)
    k1 = keys_ref[0, 0].astype(BF16)
    k2 = keys_ref[0, 1].astype(BF16)
    s1 = lax.dot_general(k1, q[:, :PEER_HALF], _NT, preferred_element_type=F32)
    s2 = lax.dot_general(k2, q[:, PEER_HALF:], _NT, preferred_element_type=F32)
    t1, i1 = _top16(s1)
    t2, i2 = _top16(s2)
    cand = jnp.concatenate([t1[a:a + 1] + t2 for a in range(PEER_TOPK)], axis=0)
    cidx = jnp.concatenate([i1[a:a + 1] * N_KEYS + i2 for a in range(PEER_TOPK)], axis=0)
    top, eidx = _top16(cand, cidx)
    e = jnp.exp(top - top[0:1])
    idx_ref[0] = eidx * PEER_ROWS
    g_ref[0] = e / jnp.sum(e, axis=0, keepdims=True)


def peer_topk(q, keys, *, tt):
    s = q.shape[0]
    return pl.pallas_call(
        _peer_topk_kernel,
        grid=(s // tt, PEER_HEADS),
        in_specs=[
            pl.BlockSpec((tt, 2 * PEER_HALF), lambda i, h: (i, h)),
            pl.BlockSpec((1, 2, N_KEYS, PEER_HALF), lambda i, h: (h, 0, 0, 0)),
        ],
        out_specs=[pl.BlockSpec((1, PEER_TOPK, tt), lambda i, h: (h, 0, i)),
                   pl.BlockSpec((1, PEER_TOPK, tt), lambda i, h: (h, 0, i))],
        out_shape=[jax.ShapeDtypeStruct((PEER_HEADS, PEER_TOPK, s), I32),
                   jax.ShapeDtypeStruct((PEER_HEADS, PEER_TOPK, s), F32)],
        compiler_params=_cparams("parallel", "parallel"),
        name="peer_topk",
    )(q, keys)


PEER_NBUF = 8
PEER_AHEAD = 5
PEER_GROUP = 8
PEER_UROWS = 16
PEER_ROWS = 2 * PEER_UROWS
PEER_PITCH = 40
HI_MASK = 0xFFFF0000


def _peer_mix_kernel(idx_ref, idx_next_ref, gt_ref, x_ref, ng_ref, tab_ref, o_ref,
                     idx_smem, xn_ref, xb_ref, ps_ref, g_ref, buf, sems, idx_sem, *, tb, half):
    step = pl.program_id(0)
    cp0 = pltpu.make_async_copy(idx_ref, idx_smem.at[:, pl.ds(0, tb)], idx_sem.at[0])
    cp1 = pltpu.make_async_copy(idx_next_ref, idx_smem.at[:, pl.ds(tb, LANES)], idx_sem.at[1])
    cp0.start()
    cp1.start()
    xn_ref[...] = _rms(x_ref[...], ng_ref[...])
    g_ref[...] = gt_ref[...].T
    cp0.wait()
    cp1.wait()

    def issue_rows(t, slot, e0, e1):
        for e in range(e0, e1):
            src = tab_ref.at[pl.ds(pl.multiple_of(idx_smem[e, t], PEER_ROWS), PEER_ROWS), :]
            dst = buf.at[slot, pl.ds(e * PEER_PITCH, PEER_ROWS), :]
            pltpu.make_async_copy(src, dst, sems.at[slot]).start(priority=e % 2)

    def wait(slot):
        nrow = PEER_SEL * PEER_ROWS
        pltpu.make_async_copy(tab_ref.at[pl.ds(0, nrow), :], buf.at[slot, pl.ds(0, nrow), :], sems.at[slot]).wait()

    ones8 = jnp.ones((8, LANES), BF16)
    er = lax.broadcasted_iota(I32, (PEER_SEL, 2 * PEER_SEL), 0)
    ec = lax.broadcasted_iota(I32, (PEER_SEL, 2 * PEER_SEL), 1)
    expand = (ec // 2 == er).astype(BF16)
    lane = lax.broadcasted_iota(I32, (8, 2 * PEER_SEL), 1)
    sub = lax.broadcasted_iota(I32, (8, 2 * PEER_SEL), 0)
    pick = sub == (lane % 2)
    ngrp = PEER_SEL // PEER_GROUP

    def rows(slot, grp, r):
        return buf[slot, pl.ds(grp * PEER_GROUP * PEER_PITCH + r, PEER_GROUP, stride=PEER_PITCH), :]

    def key_group(slot, grp, r0, r1):
        acc = None
        for r in range(r0, r1):
            w = rows(slot, grp, r)
            lo = pltpu.bitcast(w << 16, F32)
            hi = pltpu.bitcast(w & jnp.uint32(HI_MASK), F32)
            term = lo * xb_ref[:, r * LANES:(r + 1) * LANES] + hi * xb_ref[:, half + r * LANES:half + (r + 1) * LANES]
            acc = term if acc is None else acc + term
        return acc

    def key_scores(par):
        psum = ps_ref[par]
        p_hi = psum.astype(BF16)
        p_lo = (psum - p_hi.astype(F32)).astype(BF16)
        return (lax.dot_general(ones8, p_hi, _NT, preferred_element_type=F32)
                + lax.dot_general(ones8, p_lo, _NT, preferred_element_type=F32))

    def gate_weights(t, act):
        wgt = jax.nn.gelu(act) * g_ref[pl.ds(t, 1), :]
        wexp = jnp.dot(wgt.astype(BF16), expand, preferred_element_type=F32)
        return jnp.where(pick, wexp, 0.0).astype(BF16)

    def value_dot(slot, r, w8):
        vw = jnp.concatenate([rows(slot, grp, PEER_UROWS + r) for grp in range(ngrp)], axis=0)
        o = jnp.dot(w8, pltpu.bitcast(vw, BF16), preferred_element_type=F32)
        return o[0:1], o[1:2]

    def finish(t, lo, hi):
        o_ref[pl.ds(t, 1), :] = x_ref[pl.ds(t, 1), :] + jnp.concatenate(lo + hi, axis=1)

    def region(t, k, w8_prev):
        wait(k)
        xb_ref[...] = jnp.broadcast_to(xn_ref[pl.ds(t, 1), :], xb_ref.shape)
        t_gate = jnp.maximum(t - 1, 0)
        t_mix = jnp.maximum(t - 2, 0)
        parts, lo, hi = [], [], []
        for part in range(2):
            if part == 0:
                act = key_scores((k - 1) % 2)
            else:
                w8 = gate_weights(t_gate, act)
                w8 = jnp.where(t >= 1, w8, jnp.zeros_like(w8))
            for r in range(part * PEER_UROWS // 2, (part + 1) * PEER_UROWS // 2):
                a, b = value_dot((k - 2) % PEER_NBUF, r, w8_prev)
                lo.append(a)
                hi.append(b)
            accs = [None] * (ngrp // 2)
            for r0 in range(0, PEER_UROWS, PEER_UROWS // 2):
                for gi in range(ngrp // 2):
                    a = key_group(k, part * ngrp // 2 + gi, r0, r0 + PEER_UROWS // 2)
                    accs[gi] = a if accs[gi] is None else accs[gi] + a
            parts.extend(accs)
            issue_rows(t + PEER_AHEAD, (k + PEER_AHEAD) % PEER_NBUF, part * PEER_SEL // 2, (part + 1) * PEER_SEL // 2)
        ps_ref[k % 2] = jnp.concatenate(parts, axis=0)
        finish(t_mix, lo, hi)
        return w8

    @pl.when(step == 0)
    def _():
        ps_ref[...] = jnp.zeros_like(ps_ref)
        for k in range(PEER_NBUF - 2, PEER_NBUF):
            buf[k] = jnp.zeros(buf.shape[1:], U32)

        def prime(k, _):
            issue_rows(k, k, 0, PEER_SEL)
            return 0

        lax.fori_loop(0, PEER_AHEAD, prime, 0)

    def body(i, w8):
        for k in range(PEER_NBUF):
            w8 = region(i * PEER_NBUF + k, k, w8)
        return w8

    w8 = lax.fori_loop(0, tb // PEER_NBUF, body, jnp.zeros((8, 2 * PEER_SEL), BF16))
    lo, hi = zip(*[value_dot(PEER_NBUF - 2, r, w8) for r in range(PEER_UROWS)])
    finish(tb - 2, list(lo), list(hi))
    w8 = gate_weights(tb - 1, key_scores((PEER_NBUF - 1) % 2))
    lo, hi = zip(*[value_dot(PEER_NBUF - 1, r, w8) for r in range(PEER_UROWS)])
    finish(tb - 1, list(lo), list(hi))

    @pl.when(step == pl.num_programs(0) - 1)
    def _():
        for k in range(PEER_AHEAD):
            wait(k)


def peer_mix(eidx, gt, x, norm_g, table, *, tb):
    s, d = x.shape
    half = d // 2
    assert half == PEER_UROWS * LANES and tb % PEER_NBUF == 0 and tb // PEER_NBUF >= 2 and tb % LANES == 0
    assert PEER_AHEAD <= LANES
    eidx = jnp.pad(eidx, ((0, 0), (0, LANES)))
    nb = tb // LANES
    return pl.pallas_call(
        functools.partial(_peer_mix_kernel, tb=tb, half=half),
        grid=(s // tb,),
        in_specs=[
            pl.BlockSpec((PEER_SEL, tb), lambda i: (0, i)),
            pl.BlockSpec((PEER_SEL, LANES), lambda i: (0, (i + 1) * nb)),
            pl.BlockSpec((PEER_SEL, tb), lambda i: (0, i)),
            pl.BlockSpec((tb, d), lambda i: (i, 0)),
            pl.BlockSpec((1, d), lambda i: (0, 0)),
            pl.BlockSpec(memory_space=pl.ANY),
        ],
        out_specs=pl.BlockSpec((tb, d), lambda i: (i, 0)),
        out_shape=jax.ShapeDtypeStruct((s, d), F32),
        scratch_shapes=[
            pltpu.SMEM((PEER_SEL, tb + LANES), I32),
            pltpu.VMEM((tb, d), F32),
            pltpu.VMEM((8, d), F32),
            pltpu.VMEM((2, PEER_SEL, LANES), F32),
            pltpu.VMEM((tb, PEER_SEL), F32),
            pltpu.VMEM((PEER_NBUF, PEER_SEL * PEER_PITCH, LANES), U32),
            pltpu.SemaphoreType.DMA((PEER_NBUF,)),
            pltpu.SemaphoreType.DMA((2,)),
        ],
        compiler_params=_cparams("arbitrary"),
        name="peer_mix",
    )(eidx, eidx, gt, x, norm_g.reshape(1, d), table)


def _pack_expert_table(u, v):
    def pack(w):
        half = w.shape[1] // 2
        b = lax.bitcast_convert_type(w.astype(BF16), jnp.uint16).astype(U32)
        return b[:, :half] | (b[:, half:] << 16)
    return jnp.concatenate([pack(u), pack(v)], axis=1).reshape(-1, LANES)


def _layer(x, h, mem, lam_init, p):
    s, d = x.shape
    n_qkvo = 2 * ML_HEADS * ML_QK + 2 * ML_WIDTH
    proj = matmul(h, p["w_main"], tm=1024, tn=1024, out_dtype=BF16, name="in_proj")
    gates_if = matmul(h, p["w_if"], tm=1024, tn=LANES, out_dtype=F32, bias=p["b_if"], name="if_proj")
    kt = proj[:, ML_HEADS * ML_QK:2 * ML_HEADS * ML_QK].T
    if_row = gates_if[:, :2 * ML_HEADS].T
    hm = mlstm(proj, kt, gates_if, if_row, p["ml_norm"], L=128)
    hc = short_conv(proj, p["conv_w"], tr=512)
    qn, kn = da_prep(proj, p["da_q_norm"], p["da_k_norm"], tr=512)
    hd = diff_attention(qn, kn, proj, p["da_lambda"], p["da_subln"], tq=512, lam_init=lam_init)
    merged = gated_merge(h, hm, hc, hd, p["w_gate"], p["b_gate"], p["w_branch"], tm=512, tn=256)
    x = matmul(merged, p["w_out"], tm=512, tn=1024, out_dtype=F32, residual=x, name="out_proj")
    k, v = xa_kv(mem, p["mem_norm"], p["xa_wkv"], p["xa_k_norm"])
    x, xn = cross_attention(x, p["xa_norm"], p["xa_wq"], p["xa_q_norm"], k, v, p["xa_wo"], p["ffn_norm"], tm=256)
    q = matmul(xn, p["peer_wq"], tm=1024, tn=1024, out_dtype=F32, name="peer_q")
    eidx, g = peer_topk(q, p["peer_keys"], tt=512)
    return peer_mix(eidx.reshape(PEER_SEL, s), g.reshape(PEER_SEL, s), x, p["ffn_norm"], p["table"], tb=256)


def kernel(x, mem, mix_norm, w_in, b_if, ml_norm, conv_w, da_q_norm, da_k_norm, da_lambda, da_subln, w_gate, b_gate, w_branch, w_out, xa_norm, mem_norm, xa_wq, xa_wkv, xa_q_norm, xa_k_norm, xa_wo, ffn_norm, peer_wq, peer_keys, peer_u, peer_v):
    depth = w_in.shape[0]
    d = x.shape[-1]
    xs = x.reshape(-1, d)
    mems = mem.reshape(-1, d)
    n_qkvo = 2 * ML_HEADS * ML_QK + 2 * ML_WIDTH
    for l in range(depth):
        wl = w_in[l]
        w_if = jnp.pad(wl[:, n_qkvo:n_qkvo + 2 * ML_HEADS], ((0, 0), (0, LANES - 2 * ML_HEADS)))
        bias_if = jnp.pad(b_if[l].reshape(1, 2 * ML_HEADS), ((0, 0), (0, LANES - 2 * ML_HEADS)))
        p = dict(
            w_main=jnp.concatenate([wl[:, :n_qkvo], wl[:, n_qkvo + 2 * ML_HEADS:]], axis=1).astype(BF16),
            w_if=w_if.astype(BF16), b_if=bias_if,
            ml_norm=ml_norm[l], conv_w=conv_w[l], da_q_norm=da_q_norm[l], da_k_norm=da_k_norm[l],
            da_lambda=da_lambda[l], da_subln=da_subln[l],
            w_gate=w_gate[l].astype(BF16), b_gate=b_gate[l].reshape(1, -1), w_branch=w_branch[l].astype(BF16),
            w_out=w_out[l].astype(BF16), xa_norm=xa_norm[l], mem_norm=mem_norm[l], xa_wq=xa_wq[l].astype(BF16),
            xa_wkv=xa_wkv[l].astype(BF16), xa_q_norm=xa_q_norm[l], xa_k_norm=xa_k_norm[l],
            xa_wo=xa_wo[l].astype(BF16), ffn_norm=ffn_norm[l], peer_wq=peer_wq[l].astype(BF16),
            peer_keys=peer_keys[l], table=_pack_expert_table(peer_u[l], peer_v[l]),
        )
        h = rmsnorm(xs, mix_norm[l], tr=256)
        lam_init = 0.8 - 0.6 * math.exp(-0.3 * l)
        xs = _layer(xs, h, mems, lam_init, p)
    return xs.reshape(x.shape)
```

```python
import functools
import math

import jax
import jax.numpy as jnp
import numpy as np
from jax import lax
from jax.experimental import pallas as pl
from jax.experimental.pallas import tpu as pltpu

F32 = jnp.float32
BF16 = jnp.bfloat16
I32 = jnp.int32
U32 = jnp.uint32

EPS = 1e-6
CHUNK = 64

ML_HEADS, ML_QK, ML_V = 8, 128, 256
ML_WIDTH = ML_HEADS * ML_V
CV_WIDTH, CV_K = 1024, 3
DA_HEADS, DA_QK, DA_V = 8, 64, 128
DA_WIDTH = DA_HEADS * DA_V
XA_HEADS, XA_DIM = 4, 128
XA_WIDTH = XA_HEADS * XA_DIM
PEER_HEADS, N_KEYS, PEER_TOPK = 8, 128, 16
PEER_HALF = 128
PEER_SEL = PEER_HEADS * PEER_TOPK

V7X_VMEM_BYTES = 64 * 1024 * 1024
VMEM_LIMIT = V7X_VMEM_BYTES - 8 * 1024 * 1024
LANES = 128

_NT = (((1,), (1,)), ((), ()))


def _cparams(*sem):
    return pltpu.CompilerParams(dimension_semantics=sem, vmem_limit_bytes=VMEM_LIMIT)


def _rms(x, g):
    return x * lax.rsqrt(jnp.mean(x * x, axis=-1, keepdims=True) + EPS) * g


def _rmsnorm_kernel(x_ref, g_ref, o_ref):
    o_ref[...] = _rms(x_ref[...].astype(F32), g_ref[...]).astype(o_ref.dtype)


def rmsnorm(x, g, *, tr, out_dtype=BF16):
    rows, width = x.shape
    return pl.pallas_call(
        _rmsnorm_kernel,
        grid=(rows // tr,),
        in_specs=[pl.BlockSpec((tr, width), lambda i: (i, 0)), pl.BlockSpec((1, width), lambda i: (0, 0))],
        out_specs=pl.BlockSpec((tr, width), lambda i: (i, 0)),
        out_shape=jax.ShapeDtypeStruct((rows, width), out_dtype),
        compiler_params=_cparams("parallel"),
        name="rmsnorm",
    )(x, g.reshape(1, width))


def _mm_kernel(a_ref, b_ref, o_ref):
    o_ref[...] = jnp.dot(a_ref[...], b_ref[...], preferred_element_type=F32).astype(o_ref.dtype)


def _mm_bias_kernel(a_ref, b_ref, bias_ref, o_ref):
    acc = jnp.dot(a_ref[...], b_ref[...], preferred_element_type=F32)
    o_ref[...] = (acc + bias_ref[...]).astype(o_ref.dtype)


def _mm_res_kernel(a_ref, b_ref, r_ref, o_ref):
    acc = jnp.dot(a_ref[...], b_ref[...], preferred_element_type=F32)
    o_ref[...] = (r_ref[...] + acc).astype(o_ref.dtype)


def matmul(a, b, *, tm, tn, out_dtype, bias=None, residual=None, name="matmul"):
    m, k = a.shape
    n = b.shape[1]
    in_specs = [pl.BlockSpec((tm, k), lambda i, j: (i, 0)), pl.BlockSpec((k, tn), lambda i, j: (0, j))]
    args = [a, b]
    kern = _mm_kernel
    if bias is not None:
        kern = _mm_bias_kernel
        in_specs.append(pl.BlockSpec((1, tn), lambda i, j: (0, j)))
        args.append(bias)
    if residual is not None:
        kern = _mm_res_kernel
        in_specs.append(pl.BlockSpec((tm, tn), lambda i, j: (i, j)))
        args.append(residual)
    return pl.pallas_call(
        kern,
        grid=(m // tm, n // tn),
        in_specs=in_specs,
        out_specs=pl.BlockSpec((tm, tn), lambda i, j: (i, j)),
        out_shape=jax.ShapeDtypeStruct((m, n), out_dtype),
        compiler_params=_cparams("parallel", "arbitrary"),
        name=name,
    )(*args)


def _log_sigmoid(x):
    return jnp.minimum(x, 0.0) - jnp.log(1.0 + jnp.exp(-jnp.abs(x)))


def _mlstm_kernel(q_ref, kt_ref, k_ref, v_ref, o_ref, ifc_ref, ifr_ref, g_ref, out_ref, ct_ref, n_ref, m_ref, *, L):
    c = pl.program_id(0)

    @pl.when(c == 0)
    def _():
        ct_ref[...] = jnp.zeros_like(ct_ref)
        n_ref[...] = jnp.zeros_like(n_ref)
        m_ref[...] = jnp.zeros_like(m_ref)

    row = lax.broadcasted_iota(I32, (L, L), 0)
    col = lax.broadcasted_iota(I32, (L, L), 1)
    tril = col <= row
    ltri = tril.astype(F32)
    utri = (row <= col).astype(F32)
    ifc = ifc_ref[...]
    ifr = ifr_ref[...]
    b_cols = jnp.dot(ltri, _log_sigmoid(ifc), precision=lax.Precision.HIGHEST, preferred_element_type=F32)
    b_rows = jnp.dot(_log_sigmoid(ifr), utri, precision=lax.Precision.HIGHEST, preferred_element_type=F32)
    scale = ML_QK ** -0.5
    for h in range(ML_HEADS):
        q = q_ref[:, h * ML_QK:(h + 1) * ML_QK]
        kt = kt_ref[h * ML_QK:(h + 1) * ML_QK, :]
        k = k_ref[:, h * ML_QK:(h + 1) * ML_QK]
        v = v_ref[:, h * ML_V:(h + 1) * ML_V]
        bc = b_cols[:, ML_HEADS + h:ML_HEADS + h + 1]
        br = b_rows[ML_HEADS + h:ML_HEADS + h + 1, :]
        ic = ifc[:, h:h + 1]
        ir = ifr[h:h + 1, :]
        m_prev = m_ref[h:h + 1, 0:1]
        n_prev = n_ref[h:h + 1, :]
        ct_prev = ct_ref[h]

        d = jnp.where(tril, bc - br + ir, -jnp.inf)
        inter = bc + m_prev
        mt = jnp.maximum(inter, jnp.max(d, axis=1, keepdims=True))
        qk = jnp.dot(q, kt, preferred_element_type=F32) * scale
        w = jnp.exp(d - mt) * qk
        s_prev = jnp.exp(inter - mt)
        num = jnp.dot(w.astype(BF16), v, preferred_element_type=F32)
        num = num + s_prev * (jnp.dot(q, ct_prev.astype(BF16), preferred_element_type=F32) * scale)
        qn = jnp.sum(q.astype(F32) * n_prev, axis=1, keepdims=True) * scale
        den = jnp.sum(w, axis=1, keepdims=True) + s_prev * qn
        hh = num / jnp.maximum(jnp.abs(den), jnp.exp(-mt))

        bl = bc[L - 1:L, :]
        a_col = bl - bc + ic
        a_row = bl - br + ir
        m_new = jnp.maximum(bl + m_prev, jnp.max(a_row, axis=1, keepdims=True))
        decay = jnp.exp(bl + m_prev - m_new)
        wa = jnp.exp(a_col - m_new)
        vf = v.astype(F32)
        ct_ref[h] = decay * ct_prev + jnp.dot(kt, (wa * vf).astype(BF16), preferred_element_type=F32)
        n_ref[h:h + 1, :] = decay * n_prev + jnp.sum(wa * k.astype(F32), axis=0, keepdims=True)
        m_ref[h:h + 1, :] = jnp.broadcast_to(m_new, (1, LANES))

        gain = g_ref[:, h * ML_V:(h + 1) * ML_V]
        og = o_ref[:, h * ML_V:(h + 1) * ML_V].astype(F32)
        out_ref[:, h * ML_V:(h + 1) * ML_V] = (_rms(hh, gain) * jax.nn.sigmoid(og)).astype(out_ref.dtype)


def mlstm(proj, kt, if_col, if_row, gain, *, L):
    s = proj.shape[0]
    qk_w = ML_HEADS * ML_QK
    return pl.pallas_call(
        functools.partial(_mlstm_kernel, L=L),
        grid=(s // L,),
        in_specs=[
            pl.BlockSpec((L, qk_w), lambda c: (c, 0)),
            pl.BlockSpec((qk_w, L), lambda c: (0, c)),
            pl.BlockSpec((L, qk_w), lambda c: (c, 1)),
            pl.BlockSpec((L, ML_WIDTH), lambda c: (c, 1)),
            pl.BlockSpec((L, ML_WIDTH), lambda c: (c, 2)),
            pl.BlockSpec((L, LANES), lambda c: (c, 0)),
            pl.BlockSpec((2 * ML_HEADS, L), lambda c: (0, c)),
            pl.BlockSpec((1, ML_WIDTH), lambda c: (0, 0)),
        ],
        out_specs=pl.BlockSpec((L, ML_WIDTH), lambda c: (c, 0)),
        out_shape=jax.ShapeDtypeStruct((s, ML_WIDTH), BF16),
        scratch_shapes=[
            pltpu.VMEM((ML_HEADS, ML_QK, ML_V), F32),
            pltpu.VMEM((ML_HEADS, ML_QK), F32),
            pltpu.VMEM((ML_HEADS, LANES), F32),
        ],
        compiler_params=_cparams("arbitrary"),
        name="mlstm",
    )(proj, kt, proj, proj, proj, if_col, if_row, gain.reshape(1, ML_WIDTH))


def _conv_kernel(b_ref, c_ref, h_ref, w_ref, o_ref, carry_ref):
    @pl.when(pl.program_id(0) == 0)
    def _():
        carry_ref[...] = jnp.zeros_like(carry_ref)

    u = c_ref[...].astype(F32) * h_ref[...].astype(F32)
    rows = u.shape[0]
    ext = jnp.concatenate([carry_ref[...], u], axis=0)
    u1 = ext[7:7 + rows]
    u2 = ext[6:6 + rows]
    w = w_ref[...]
    y = w[0:1] * u2 + w[1:2] * u1 + w[2:3] * u
    o_ref[...] = (b_ref[...].astype(F32) * y).astype(o_ref.dtype)
    carry_ref[...] = u[rows - 8:rows]


def short_conv(proj, w, *, tr):
    s = proj.shape[0]
    base = 0
    return pl.pallas_call(
        _conv_kernel,
        grid=(s // tr,),
        in_specs=[
            pl.BlockSpec((tr, CV_WIDTH), lambda i: (i, base)),
            pl.BlockSpec((tr, CV_WIDTH), lambda i: (i, base + 1)),
            pl.BlockSpec((tr, CV_WIDTH), lambda i: (i, base + 2)),
            pl.BlockSpec((CV_K, CV_WIDTH), lambda i: (0, 0)),
        ],
        out_specs=pl.BlockSpec((tr, CV_WIDTH), lambda i: (i, 0)),
        out_shape=jax.ShapeDtypeStruct((s, CV_WIDTH), BF16),
        scratch_shapes=[pltpu.VMEM((8, CV_WIDTH), F32)],
        compiler_params=_cparams("arbitrary"),
        name="short_conv",
    )(proj, proj, proj, w)


def _da_prep_kernel(q_ref, k_ref, qg_ref, kg_ref, qo_ref, ko_ref):
    r = lax.broadcasted_iota(I32, (LANES, LANES), 0) // DA_QK
    c = lax.broadcasted_iota(I32, (LANES, LANES), 1) // DA_QK
    seg = (r == c).astype(F32)

    def norm(x_ref, g_ref, o_ref, scale):
        for j in range(x_ref.shape[1] // LANES):
            x = x_ref[:, j * LANES:(j + 1) * LANES].astype(F32)
            ss = jnp.dot(x * x, seg, precision=lax.Precision.HIGHEST, preferred_element_type=F32)
            y = x * lax.rsqrt(ss * (1.0 / DA_QK) + EPS) * g_ref[:, j * LANES:(j + 1) * LANES]
            o_ref[:, j * LANES:(j + 1) * LANES] = (y * scale).astype(o_ref.dtype)

    norm(q_ref, qg_ref, qo_ref, DA_QK ** -0.5)
    norm(k_ref, kg_ref, ko_ref, 1.0)


def da_prep(proj, q_g, k_g, *, tr):
    s = proj.shape[0]
    w = DA_HEADS * 2 * DA_QK
    base = 3 * CV_WIDTH // w
    qg = jnp.tile(q_g, w // DA_QK).reshape(1, w)
    kg = jnp.tile(k_g, w // DA_QK).reshape(1, w)
    return pl.pallas_call(
        _da_prep_kernel,
        grid=(s // tr,),
        in_specs=[
            pl.BlockSpec((tr, w), lambda i: (i, base)),
            pl.BlockSpec((tr, w), lambda i: (i, base + 1)),
            pl.BlockSpec((1, w), lambda i: (0, 0)),
            pl.BlockSpec((1, w), lambda i: (0, 0)),
        ],
        out_specs=[pl.BlockSpec((tr, w), lambda i: (i, 0)), pl.BlockSpec((tr, w), lambda i: (i, 0))],
        out_shape=[jax.ShapeDtypeStruct((s, w), BF16), jax.ShapeDtypeStruct((s, w), BF16)],
        compiler_params=_cparams("parallel"),
        name="da_prep",
    )(proj, proj, qg, kg)


def _da_kernel(q_ref, k_ref, v_ref, lam_ref, g_ref, o_ref, *, tq, lam_init):
    qi = pl.program_id(1)
    q = q_ref[...]
    lane = lax.broadcasted_iota(I32, q.shape, 1)
    zero = jnp.zeros_like(q)
    q1 = jnp.where(lane < DA_QK, q, zero)
    q2 = jnp.where(lane >= DA_QK, q, zero)

    def scores(kb):
        return (lax.dot_general(q1, kb, _NT, preferred_element_type=F32),
                lax.dot_general(q2, kb, _NT, preferred_element_type=F32))

    def update(carry, s, vb):
        m, l, acc = carry
        m_new = jnp.maximum(m, jnp.max(s, axis=1, keepdims=True))
        alpha = jnp.exp(m - m_new)
        p = jnp.exp(s - m_new)
        l = alpha * l + jnp.sum(p, axis=1, keepdims=True)
        acc = alpha * acc + jnp.dot(p.astype(BF16), vb, preferred_element_type=F32)
        return m_new, l, acc

    def body(ki, carry):
        c1, c2 = carry
        off = pl.multiple_of(ki * tq, tq)
        kb = k_ref[pl.ds(off, tq), :]
        vb = v_ref[pl.ds(off, tq), :]
        s1, s2 = scores(kb)
        return update(c1, s1, vb), update(c2, s2, vb)

    def init():
        return (jnp.full((tq, 1), -jnp.inf, F32), jnp.zeros((tq, 1), F32), jnp.zeros((tq, DA_V), F32))

    c1, c2 = lax.fori_loop(0, qi, body, (init(), init()))
    off = pl.multiple_of(qi * tq, tq)
    kb = k_ref[pl.ds(off, tq), :]
    vb = v_ref[pl.ds(off, tq), :]
    s1, s2 = scores(kb)
    rq = lax.broadcasted_iota(I32, (tq, tq), 0) // CHUNK
    ck = lax.broadcasted_iota(I32, (tq, tq), 1) // CHUNK
    vis = ck <= rq
    _, l1, a1 = update(c1, jnp.where(vis, s1, -jnp.inf), vb)
    _, l2, a2 = update(c2, jnp.where(vis, s2, -jnp.inf), vb)

    lp = lam_ref[...]
    lam = (jnp.exp(jnp.sum(lp[0:1] * lp[1:2], axis=1, keepdims=True))
           - jnp.exp(jnp.sum(lp[2:3] * lp[3:4], axis=1, keepdims=True)) + lam_init)
    o = a1 / l1 - lam * (a2 / l2)
    o_ref[...] = (_rms(o, g_ref[...]) * (1.0 - lam_init)).astype(o_ref.dtype)


def diff_attention(qn, kn, proj, lam_p, subln_g, *, tq, lam_init):
    s = qn.shape[0]
    vbase = (3 * CV_WIDTH + 2 * DA_HEADS * 2 * DA_QK) // DA_V
    return pl.pallas_call(
        functools.partial(_da_kernel, tq=tq, lam_init=lam_init),
        grid=(DA_HEADS, s // tq),
        in_specs=[
            pl.BlockSpec((tq, 2 * DA_QK), lambda h, i: (i, h)),
            pl.BlockSpec((s, 2 * DA_QK), lambda h, i: (0, h)),
            pl.BlockSpec((s, DA_V), lambda h, i: (0, vbase + h)),
            pl.BlockSpec((4, DA_QK), lambda h, i: (0, 0)),
            pl.BlockSpec((1, DA_V), lambda h, i: (0, 0)),
        ],
        out_specs=pl.BlockSpec((tq, DA_V), lambda h, i: (i, h)),
        out_shape=jax.ShapeDtypeStruct((s, DA_WIDTH), BF16),
        compiler_params=_cparams("parallel", "arbitrary"),
        name="diff_attention",
    )(qn, kn, proj, lam_p, subln_g.reshape(1, DA_V))


def _merge_kernel(h_ref, hm_ref, hc_ref, hd_ref, wg0, wg1, wg2, bg0, bg1, bg2, wbm, wbc, wbd, o_ref):
    h = h_ref[...]

    def gated(wg, bg, hb_ref, wb):
        gate = jax.nn.sigmoid(jnp.dot(h, wg[...], preferred_element_type=F32) + bg[...])
        return gate * jnp.dot(hb_ref[...], wb[...], preferred_element_type=F32)

    out = gated(wg0, bg0, hm_ref, wbm) + gated(wg1, bg1, hc_ref, wbc) + gated(wg2, bg2, hd_ref, wbd)
    o_ref[...] = out.astype(o_ref.dtype)


def gated_merge(h, hm, hc, hd, w_gate, b_gate, w_branch, *, tm, tn):
    s, d = h.shape
    nj = d // tn
    r_c = ML_WIDTH // CV_WIDTH
    r_d = (ML_WIDTH + CV_WIDTH) // DA_WIDTH

    def gspec(b):
        return pl.BlockSpec((d, tn), lambda i, j: (0, j + b * nj))

    def bspec(b):
        return pl.BlockSpec((1, tn), lambda i, j: (0, j + b * nj))

    return pl.pallas_call(
        _merge_kernel,
        grid=(s // tm, nj),
        in_specs=[
            pl.BlockSpec((tm, d), lambda i, j: (i, 0)),
            pl.BlockSpec((tm, ML_WIDTH), lambda i, j: (i, 0)),
            pl.BlockSpec((tm, CV_WIDTH), lambda i, j: (i, 0)),
            pl.BlockSpec((tm, DA_WIDTH), lambda i, j: (i, 0)),
            gspec(0), gspec(1), gspec(2), bspec(0), bspec(1), bspec(2),
            pl.BlockSpec((ML_WIDTH, tn), lambda i, j: (0, j)),
            pl.BlockSpec((CV_WIDTH, tn), lambda i, j: (r_c, j)),
            pl.BlockSpec((DA_WIDTH, tn), lambda i, j: (r_d, j)),
        ],
        out_specs=pl.BlockSpec((tm, tn), lambda i, j: (i, j)),
        out_shape=jax.ShapeDtypeStruct((s, d), BF16),
        compiler_params=_cparams("parallel", "arbitrary"),
        name="gated_merge",
    )(h, hm, hc, hd, w_gate, w_gate, w_gate, b_gate, b_gate, b_gate, w_branch, w_branch, w_branch)


def _xa_kv_kernel(mem_ref, mg_ref, wkv_ref, kg_ref, k_ref, v_ref):
    mn = _rms(mem_ref[...], mg_ref[...]).astype(BF16)
    kv = jnp.dot(mn, wkv_ref[...], preferred_element_type=F32)
    for hh in range(XA_HEADS):
        kh = kv[:, hh * XA_DIM:(hh + 1) * XA_DIM]
        k_ref[:, hh * XA_DIM:(hh + 1) * XA_DIM] = _rms(kh, kg_ref[...]).astype(k_ref.dtype)
    v_ref[...] = kv[:, XA_WIDTH:].astype(v_ref.dtype)


def xa_kv(mem, mem_g, wkv, k_g):
    m, d = mem.shape
    return pl.pallas_call(
        _xa_kv_kernel,
        out_shape=[jax.ShapeDtypeStruct((m, XA_WIDTH), BF16), jax.ShapeDtypeStruct((m, XA_WIDTH), BF16)],
        compiler_params=pltpu.CompilerParams(vmem_limit_bytes=VMEM_LIMIT),
        name="xa_kv",
    )(mem, mem_g.reshape(1, d), wkv, k_g.reshape(1, XA_DIM))


def _xa_kernel(x_ref, ng_ref, wq_ref, qg_ref, k_ref, v_ref, wo_ref, fg_ref, y_ref, yn_ref):
    x = x_ref[...]
    xn = _rms(x, ng_ref[...]).astype(BF16)
    q = jnp.dot(xn, wq_ref[...], preferred_element_type=F32)
    outs = []
    for hh in range(XA_HEADS):
        sl = slice(hh * XA_DIM, (hh + 1) * XA_DIM)
        qh = _rms(q[:, sl], qg_ref[...]).astype(BF16)
        s = lax.dot_general(qh, k_ref[:, sl], _NT, preferred_element_type=F32) * (XA_DIM ** -0.5)
        p = jnp.exp(s - jnp.max(s, axis=1, keepdims=True))
        p = p / jnp.sum(p, axis=1, keepdims=True)
        outs.append(jnp.dot(p.astype(BF16), v_ref[:, sl], preferred_element_type=F32))
    o = jnp.concatenate(outs, axis=1).astype(BF16)
    y = x + jnp.dot(o, wo_ref[...], preferred_element_type=F32)
    y_ref[...] = y
    yn_ref[...] = _rms(y, fg_ref[...]).astype(yn_ref.dtype)


def cross_attention(x, norm_g, wq, q_g, k, v, wo, next_g, *, tm):
    s, d = x.shape
    m = k.shape[0]
    const = lambda i: (0, 0)
    return pl.pallas_call(
        _xa_kernel,
        grid=(s // tm,),
        in_specs=[
            pl.BlockSpec((tm, d), lambda i: (i, 0)),
            pl.BlockSpec((1, d), const),
            pl.BlockSpec((d, XA_WIDTH), const),
            pl.BlockSpec((1, XA_DIM), const),
            pl.BlockSpec((m, XA_WIDTH), const),
            pl.BlockSpec((m, XA_WIDTH), const),
            pl.BlockSpec((XA_WIDTH, d), const),
            pl.BlockSpec((1, d), const),
        ],
        out_specs=[pl.BlockSpec((tm, d), lambda i: (i, 0)), pl.BlockSpec((tm, d), lambda i: (i, 0))],
        out_shape=[jax.ShapeDtypeStruct((s, d), F32), jax.ShapeDtypeStruct((s, d), BF16)],
        compiler_params=_cparams("parallel"),
        name="cross_attention",
    )(x, norm_g.reshape(1, d), wq, q_g.reshape(1, XA_DIM), k, v, wo, next_g.reshape(1, d))


def _top16(s, payload=None):
    n = s.shape[0]
    iota = lax.broadcasted_iota(I32, s.shape, 0)
    vals, sel = [], []
    for _ in range(PEER_TOPK):
        mx = jnp.max(s, axis=0, keepdims=True)
        ix = jnp.min(jnp.where(s == mx, iota, n), axis=0, keepdims=True)
        hit = iota == ix
        vals.append(mx)
        if payload is None:
            sel.append(ix)
        else:
            sel.append(jnp.max(jnp.where(hit, payload, -1), axis=0, keepdims=True))
        s = jnp.where(hit, -jnp.inf, s)
    return jnp.concatenate(vals, axis=0), jnp.concatenate(sel, axis=0)


def _peer_topk_kernel(q_ref, keys_ref, idx_ref, g_ref):
    q = q_ref[...].astype(BF16)
    k1 = keys_ref[0, 0].astype(BF16)
    k2 = keys_ref[0, 1].astype(BF16)
    s1 = lax.dot_general(k1, q[:, :PEER_HALF], _NT, preferred_element_type=F32)
    s2 = lax.dot_general(k2, q[:, PEER_HALF:], _NT, preferred_element_type=F32)
    t1, i1 = _top16(s1)
    t2, i2 = _top16(s2)
    cand = jnp.concatenate([t1[a:a + 1] + t2 for a in range(PEER_TOPK)], axis=0)
    cidx = jnp.concatenate([i1[a:a + 1] * N_KEYS + i2 for a in range(PEER_TOPK)], axis=0)
    top, eidx = _top16(cand, cidx)
    e = jnp.exp(top - top[0:1])
    idx_ref[0] = eidx * PEER_ROWS
    g_ref[0] = e / jnp.sum(e, axis=0, keepdims=True)


def peer_topk(q, keys, *, tt):
    s = q.shape[0]
    return pl.pallas_call(
        _peer_topk_kernel,
        grid=(s // tt, PEER_HEADS),
        in_specs=[
            pl.BlockSpec((tt, 2 * PEER_HALF), lambda i, h: (i, h)),
            pl.BlockSpec((1, 2, N_KEYS, PEER_HALF), lambda i, h: (h, 0, 0, 0)),
        ],
        out_specs=[pl.BlockSpec((1, PEER_TOPK, tt), lambda i, h: (h, 0, i)),
                   pl.BlockSpec((1, PEER_TOPK, tt), lambda i, h: (h, 0, i))],
        out_shape=[jax.ShapeDtypeStruct((PEER_HEADS, PEER_TOPK, s), I32),
                   jax.ShapeDtypeStruct((PEER_HEADS, PEER_TOPK, s), F32)],
        compiler_params=_cparams("parallel", "parallel"),
        name="peer_topk",
    )(q, keys)


PEER_NBUF = 8
PEER_AHEAD = 5
PEER_GROUP = 8
PEER_UROWS = 16
PEER_ROWS = 2 * PEER_UROWS
PEER_PITCH = 40
HI_MASK = 0xFFFF0000


def _peer_mix_kernel(idx_ref, idx_next_ref, gt_ref, x_ref, ng_ref, tab_ref, o_ref,
                     idx_smem, xn_ref, xb_ref, ps_ref, g_ref, buf, sems, idx_sem, *, tb, half):
    step = pl.program_id(0)
    cp0 = pltpu.make_async_copy(idx_ref, idx_smem.at[:, pl.ds(0, tb)], idx_sem.at[0])
    cp1 = pltpu.make_async_copy(idx_next_ref, idx_smem.at[:, pl.ds(tb, LANES)], idx_sem.at[1])
    cp0.start()
    cp1.start()
    xn_ref[...] = _rms(x_ref[...], ng_ref[...])
    g_ref[...] = gt_ref[...].T
    cp0.wait()
    cp1.wait()

    def issue_rows(t, slot, e0, e1):
        for e in range(e0, e1):
            src = tab_ref.at[pl.ds(pl.multiple_of(idx_smem[e, t], PEER_ROWS), PEER_ROWS), :]
            dst = buf.at[slot, pl.ds(e * PEER_PITCH, PEER_ROWS), :]
            pltpu.make_async_copy(src, dst, sems.at[slot]).start(priority=e % 2)

    def wait(slot):
        nrow = PEER_SEL * PEER_ROWS
        pltpu.make_async_copy(tab_ref.at[pl.ds(0, nrow), :], buf.at[slot, pl.ds(0, nrow), :], sems.at[slot]).wait()

    ones8 = jnp.ones((8, LANES), BF16)
    er = lax.broadcasted_iota(I32, (PEER_SEL, 2 * PEER_SEL), 0)
    ec = lax.broadcasted_iota(I32, (PEER_SEL, 2 * PEER_SEL), 1)
    expand = (ec // 2 == er).astype(BF16)
    lane = lax.broadcasted_iota(I32, (8, 2 * PEER_SEL), 1)
    sub = lax.broadcasted_iota(I32, (8, 2 * PEER_SEL), 0)
    pick = sub == (lane % 2)
    ngrp = PEER_SEL // PEER_GROUP

    def rows(slot, grp, r):
        return buf[slot, pl.ds(grp * PEER_GROUP * PEER_PITCH + r, PEER_GROUP, stride=PEER_PITCH), :]

    def key_group(slot, grp, r0, r1):
        acc = None
        for r in range(r0, r1):
            w = rows(slot, grp, r)
            lo = pltpu.bitcast(w << 16, F32)
            hi = pltpu.bitcast(w & jnp.uint32(HI_MASK), F32)
            term = lo * xb_ref[:, r * LANES:(r + 1) * LANES] + hi * xb_ref[:, half + r * LANES:half + (r + 1) * LANES]
            acc = term if acc is None else acc + term
        return acc

    def key_scores(par):
        psum = ps_ref[par]
        p_hi = psum.astype(BF16)
        p_lo = (psum - p_hi.astype(F32)).astype(BF16)
        return (lax.dot_general(ones8, p_hi, _NT, preferred_element_type=F32)
                + lax.dot_general(ones8, p_lo, _NT, preferred_element_type=F32))

    def gate_weights(t, act):
        wgt = jax.nn.gelu(act) * g_ref[pl.ds(t, 1), :]
        wexp = jnp.dot(wgt.astype(BF16), expand, preferred_element_type=F32)
        return jnp.where(pick, wexp, 0.0).astype(BF16)

    def value_dot(slot, r, w8):
        vw = jnp.concatenate([rows(slot, grp, PEER_UROWS + r) for grp in range(ngrp)], axis=0)
        o = jnp.dot(w8, pltpu.bitcast(vw, BF16), preferred_element_type=F32)
        return o[0:1], o[1:2]

    def finish(t, lo, hi):
        o_ref[pl.ds(t, 1), :] = x_ref[pl.ds(t, 1), :] + jnp.concatenate(lo + hi, axis=1)

    def region(t, k, w8_prev):
        wait(k)
        xb_ref[...] = jnp.broadcast_to(xn_ref[pl.ds(t, 1), :], xb_ref.shape)
        t_gate = jnp.maximum(t - 1, 0)
        t_mix = jnp.maximum(t - 2, 0)
        parts, lo, hi = [], [], []
        for part in range(2):
            if part == 0:
                act = key_scores((k - 1) % 2)
            else:
                w8 = gate_weights(t_gate, act)
                w8 = jnp.where(t >= 1, w8, jnp.zeros_like(w8))
            for r in range(part * PEER_UROWS // 2, (part + 1) * PEER_UROWS // 2):
                a, b = value_dot((k - 2) % PEER_NBUF, r, w8_prev)
                lo.append(a)
                hi.append(b)
            accs = [None] * (ngrp // 2)
            for r0 in range(0, PEER_UROWS, PEER_UROWS // 2):
                for gi in range(ngrp // 2):
                    a = key_group(k, part * ngrp // 2 + gi, r0, r0 + PEER_UROWS // 2)
                    accs[gi] = a if accs[gi] is None else accs[gi] + a
            parts.extend(accs)
            issue_rows(t + PEER_AHEAD, (k + PEER_AHEAD) % PEER_NBUF, part * PEER_SEL // 2, (part + 1) * PEER_SEL // 2)
        ps_ref[k % 2] = jnp.concatenate(parts, axis=0)
        finish(t_mix, lo, hi)
        return w8

    @pl.when(step == 0)
    def _():
        ps_ref[...] = jnp.zeros_like(ps_ref)
        for k in range(PEER_NBUF - 2, PEER_NBUF):
            buf[k] = jnp.zeros(buf.shape[1:], U32)

        def prime(k, _):
            issue_rows(k, k, 0, PEER_SEL)
            return 0

        lax.fori_loop(0, PEER_AHEAD, prime, 0)

    def body(i, w8):
        for k in range(PEER_NBUF):
            w8 = region(i * PEER_NBUF + k, k, w8)
        return w8

    w8 = lax.fori_loop(0, tb // PEER_NBUF, body, jnp.zeros((8, 2 * PEER_SEL), BF16))
    lo, hi = zip(*[value_dot(PEER_NBUF - 2, r, w8) for r in range(PEER_UROWS)])
    finish(tb - 2, list(lo), list(hi))
    w8 = gate_weights(tb - 1, key_scores((PEER_NBUF - 1) % 2))
    lo, hi = zip(*[value_dot(PEER_NBUF - 1, r, w8) for r in range(PEER_UROWS)])
    finish(tb - 1, list(lo), list(hi))

    @pl.when(step == pl.num_programs(0) - 1)
    def _():
        for k in range(PEER_AHEAD):
            wait(k)


def peer_mix(eidx, gt, x, norm_g, table, *, tb):
    s, d = x.shape
    half = d // 2
    assert half == PEER_UROWS * LANES and tb % PEER_NBUF == 0 and tb // PEER_NBUF >= 2 and tb % LANES == 0
    assert PEER_AHEAD <= LANES
    eidx = jnp.pad(eidx, ((0, 0), (0, LANES)))
    nb = tb // LANES
    return pl.pallas_call(
        functools.partial(_peer_mix_kernel, tb=tb, half=half),
        grid=(s // tb,),
        in_specs=[
            pl.BlockSpec((PEER_SEL, tb), lambda i: (0, i)),
            pl.BlockSpec((PEER_SEL, LANES), lambda i: (0, (i + 1) * nb)),
            pl.BlockSpec((PEER_SEL, tb), lambda i: (0, i)),
            pl.BlockSpec((tb, d), lambda i: (i, 0)),
            pl.BlockSpec((1, d), lambda i: (0, 0)),
            pl.BlockSpec(memory_space=pl.ANY),
        ],
        out_specs=pl.BlockSpec((tb, d), lambda i: (i, 0)),
        out_shape=jax.ShapeDtypeStruct((s, d), F32),
        scratch_shapes=[
            pltpu.SMEM((PEER_SEL, tb + LANES), I32),
            pltpu.VMEM((tb, d), F32),
            pltpu.VMEM((8, d), F32),
            pltpu.VMEM((2, PEER_SEL, LANES), F32),
            pltpu.VMEM((tb, PEER_SEL), F32),
            pltpu.VMEM((PEER_NBUF, PEER_SEL * PEER_PITCH, LANES), U32),
            pltpu.SemaphoreType.DMA((PEER_NBUF,)),
            pltpu.SemaphoreType.DMA((2,)),
        ],
        compiler_params=_cparams("arbitrary"),
        name="peer_mix",
    )(eidx, eidx, gt, x, norm_g.reshape(1, d), table)


def _pack_kernel(u_ref, v_ref, o_ref):
    ne, d = u_ref.shape
    half = d // 2

    def words(w_ref):
        lo = pltpu.bitcast(w_ref[:, :half].astype(BF16).astype(F32), U32)
        hi = pltpu.bitcast(w_ref[:, half:].astype(BF16).astype(F32), U32)
        return (lo >> 16) | hi

    for tab, base in ((words(u_ref), 0), (words(v_ref), PEER_UROWS)):
        for r in range(PEER_UROWS):
            o_ref[pl.ds(base + r, ne, stride=PEER_ROWS), :] = tab[:, r * LANES:(r + 1) * LANES]


def _pack_expert_table(u, v, *, te=128):
    n, d = u.shape
    assert d == 2 * PEER_UROWS * LANES
    return pl.pallas_call(
        _pack_kernel,
        grid=(n // te,),
        in_specs=[pl.BlockSpec((te, d), lambda i: (i, 0)), pl.BlockSpec((te, d), lambda i: (i, 0))],
        out_specs=pl.BlockSpec((te * PEER_ROWS, LANES), lambda i: (i, 0)),
        out_shape=jax.ShapeDtypeStruct((n * PEER_ROWS, LANES), U32),
        compiler_params=_cparams("parallel"),
        name="peer_pack",
    )(u, v)


def _layer(x, h, mem, lam_init, p):
    s, d = x.shape
    n_qkvo = 2 * ML_HEADS * ML_QK + 2 * ML_WIDTH
    proj_ml = matmul(h, p["w_ml"], tm=1024, tn=1024, out_dtype=BF16, name="in_proj_mlstm")
    proj_cd = matmul(h, p["w_cd"], tm=1024, tn=1024, out_dtype=BF16, name="in_proj_conv_da")
    gates_if = matmul(h, p["w_if"], tm=1024, tn=LANES, out_dtype=F32, bias=p["b_if"], name="if_proj")
    kt = proj_ml[:, ML_HEADS * ML_QK:2 * ML_HEADS * ML_QK].T
    if_row = gates_if[:, :2 * ML_HEADS].T
    hm = mlstm(proj_ml, kt, gates_if, if_row, p["ml_norm"], L=128)
    hc = short_conv(proj_cd, p["conv_w"], tr=512)
    qn, kn = da_prep(proj_cd, p["da_q_norm"], p["da_k_norm"], tr=512)
    hd = diff_attention(qn, kn, proj_cd, p["da_lambda"], p["da_subln"], tq=512, lam_init=lam_init)
    merged = gated_merge(h, hm, hc, hd, p["w_gate"], p["b_gate"], p["w_branch"], tm=512, tn=256)
    x = matmul(merged, p["w_out"], tm=512, tn=1024, out_dtype=F32, residual=x, name="out_proj")
    k, v = xa_kv(mem, p["mem_norm"], p["xa_wkv"], p["xa_k_norm"])
    x, xn = cross_attention(x, p["xa_norm"], p["xa_wq"], p["xa_q_norm"], k, v, p["xa_wo"], p["ffn_norm"], tm=256)
    q = matmul(xn, p["peer_wq"], tm=1024, tn=1024, out_dtype=F32, name="peer_q")
    eidx, g = peer_topk(q, p["peer_keys"], tt=512)
    return peer_mix(eidx.reshape(PEER_SEL, s), g.reshape(PEER_SEL, s), x, p["ffn_norm"], p["table"], tb=256)


def kernel(x, mem, mix_norm, w_in, b_if, ml_norm, conv_w, da_q_norm, da_k_norm, da_lambda, da_subln, w_gate, b_gate, w_branch, w_out, xa_norm, mem_norm, xa_wq, xa_wkv, xa_q_norm, xa_k_norm, xa_wo, ffn_norm, peer_wq, peer_keys, peer_u, peer_v):
    depth = w_in.shape[0]
    d = x.shape[-1]
    xs = x.reshape(-1, d)
    mems = mem.reshape(-1, d)
    n_qkvo = 2 * ML_HEADS * ML_QK + 2 * ML_WIDTH
    for l in range(depth):
        wl = w_in[l]
        w_if = jnp.pad(wl[:, n_qkvo:n_qkvo + 2 * ML_HEADS], ((0, 0), (0, LANES - 2 * ML_HEADS)))
        bias_if = jnp.pad(b_if[l].reshape(1, 2 * ML_HEADS), ((0, 0), (0, LANES - 2 * ML_HEADS)))
        p = dict(
            w_ml=wl[:, :n_qkvo].astype(BF16), w_cd=wl[:, n_qkvo + 2 * ML_HEADS:].astype(BF16),
            w_if=w_if.astype(BF16), b_if=bias_if,
            ml_norm=ml_norm[l], conv_w=conv_w[l], da_q_norm=da_q_norm[l], da_k_norm=da_k_norm[l],
            da_lambda=da_lambda[l], da_subln=da_subln[l],
            w_gate=w_gate[l].astype(BF16), b_gate=b_gate[l].reshape(1, -1), w_branch=w_branch[l].astype(BF16),
            w_out=w_out[l].astype(BF16), xa_norm=xa_norm[l], mem_norm=mem_norm[l], xa_wq=xa_wq[l].astype(BF16),
            xa_wkv=xa_wkv[l].astype(BF16), xa_q_norm=xa_q_norm[l], xa_k_norm=xa_k_norm[l],
            xa_wo=xa_wo[l].astype(BF16), ffn_norm=ffn_norm[l], peer_wq=peer_wq[l].astype(BF16),
            peer_keys=peer_keys[l], table=_pack_expert_table(peer_u[l], peer_v[l]),
        )
        h = rmsnorm(xs, mix_norm[l], tr=256)
        lam_init = 0.8 - 0.6 * math.exp(-0.3 * l)
        xs = _layer(xs, h, mems, lam_init, p)
    return xs.reshape(x.shape)
```

```python
import functools
import math

import jax
import jax.numpy as jnp
import numpy as np
from jax import lax
from jax.experimental import pallas as pl
from jax.experimental.pallas import tpu as pltpu

F32 = jnp.float32
BF16 = jnp.bfloat16
I32 = jnp.int32
U32 = jnp.uint32

EPS = 1e-6
CHUNK = 64

ML_HEADS, ML_QK, ML_V = 8, 128, 256
ML_WIDTH = ML_HEADS * ML_V
CV_WIDTH, CV_K = 1024, 3
DA_HEADS, DA_QK, DA_V = 8, 64, 128
DA_WIDTH = DA_HEADS * DA_V
XA_HEADS, XA_DIM = 4, 128
XA_WIDTH = XA_HEADS * XA_DIM
PEER_HEADS, N_KEYS, PEER_TOPK = 8, 128, 16
PEER_HALF = 128
PEER_SEL = PEER_HEADS * PEER_TOPK

V7X_VMEM_BYTES = 64 * 1024 * 1024
VMEM_LIMIT = V7X_VMEM_BYTES - 8 * 1024 * 1024
LANES = 128

_NT = (((1,), (1,)), ((), ()))


def _cparams(*sem):
    return pltpu.CompilerParams(dimension_semantics=sem, vmem_limit_bytes=VMEM_LIMIT)


def _rms(x, g):
    return x * lax.rsqrt(jnp.mean(x * x, axis=-1, keepdims=True) + EPS) * g


def _rmsnorm_kernel(x_ref, g_ref, o_ref):
    o_ref[...] = _rms(x_ref[...].astype(F32), g_ref[...]).astype(o_ref.dtype)


def rmsnorm(x, g, *, tr, out_dtype=BF16):
    rows, width = x.shape
    return pl.pallas_call(
        _rmsnorm_kernel,
        grid=(rows // tr,),
        in_specs=[pl.BlockSpec((tr, width), lambda i: (i, 0)), pl.BlockSpec((1, width), lambda i: (0, 0))],
        out_specs=pl.BlockSpec((tr, width), lambda i: (i, 0)),
        out_shape=jax.ShapeDtypeStruct((rows, width), out_dtype),
        compiler_params=_cparams("parallel"),
        name="rmsnorm",
    )(x, g.reshape(1, width))


def _mm_kernel(a_ref, b_ref, o_ref):
    o_ref[...] = jnp.dot(a_ref[...], b_ref[...], preferred_element_type=F32).astype(o_ref.dtype)


def _mm_bias_kernel(a_ref, b_ref, bias_ref, o_ref):
    acc = jnp.dot(a_ref[...], b_ref[...], preferred_element_type=F32)
    o_ref[...] = (acc + bias_ref[...]).astype(o_ref.dtype)


def _mm_res_kernel(a_ref, b_ref, r_ref, o_ref):
    acc = jnp.dot(a_ref[...], b_ref[...], preferred_element_type=F32)
    o_ref[...] = (r_ref[...] + acc).astype(o_ref.dtype)


def matmul(a, b, *, tm, tn, out_dtype, bias=None, residual=None, name="matmul"):
    m, k = a.shape
    n = b.shape[1]
    in_specs = [pl.BlockSpec((tm, k), lambda i, j: (i, 0)), pl.BlockSpec((k, tn), lambda i, j: (0, j))]
    args = [a, b]
    kern = _mm_kernel
    if bias is not None:
        kern = _mm_bias_kernel
        in_specs.append(pl.BlockSpec((1, tn), lambda i, j: (0, j)))
        args.append(bias)
    if residual is not None:
        kern = _mm_res_kernel
        in_specs.append(pl.BlockSpec((tm, tn), lambda i, j: (i, j)))
        args.append(residual)
    return pl.pallas_call(
        kern,
        grid=(m // tm, n // tn),
        in_specs=in_specs,
        out_specs=pl.BlockSpec((tm, tn), lambda i, j: (i, j)),
        out_shape=jax.ShapeDtypeStruct((m, n), out_dtype),
        compiler_params=_cparams("parallel", "arbitrary"),
        name=name,
    )(*args)


def _log_sigmoid(x):
    return jnp.minimum(x, 0.0) - jnp.log(1.0 + jnp.exp(-jnp.abs(x)))


def _mlstm_kernel(q_ref, kt_ref, k_ref, v_ref, o_ref, ifc_ref, ifr_ref, g_ref, out_ref, ct_ref, n_ref, m_ref, *, L):
    c = pl.program_id(0)

    @pl.when(c == 0)
    def _():
        ct_ref[...] = jnp.zeros_like(ct_ref)
        n_ref[...] = jnp.zeros_like(n_ref)
        m_ref[...] = jnp.zeros_like(m_ref)

    row = lax.broadcasted_iota(I32, (L, L), 0)
    col = lax.broadcasted_iota(I32, (L, L), 1)
    tril = col <= row
    ltri = tril.astype(F32)
    utri = (row <= col).astype(F32)
    ifc = ifc_ref[...]
    ifr = ifr_ref[...]
    b_cols = jnp.dot(ltri, _log_sigmoid(ifc), precision=lax.Precision.HIGHEST, preferred_element_type=F32)
    b_rows = jnp.dot(_log_sigmoid(ifr), utri, precision=lax.Precision.HIGHEST, preferred_element_type=F32)
    scale = ML_QK ** -0.5
    for h in range(ML_HEADS):
        q = q_ref[:, h * ML_QK:(h + 1) * ML_QK]
        kt = kt_ref[h * ML_QK:(h + 1) * ML_QK, :]
        k = k_ref[:, h * ML_QK:(h + 1) * ML_QK]
        v = v_ref[:, h * ML_V:(h + 1) * ML_V]
        bc = b_cols[:, ML_HEADS + h:ML_HEADS + h + 1]
        br = b_rows[ML_HEADS + h:ML_HEADS + h + 1, :]
        ic = ifc[:, h:h + 1]
        ir = ifr[h:h + 1, :]
        m_prev = m_ref[h:h + 1, 0:1]
        n_prev = n_ref[h:h + 1, :]
        ct_prev = ct_ref[h]

        d = jnp.where(tril, bc - br + ir, -jnp.inf)
        inter = bc + m_prev
        mt = jnp.maximum(inter, jnp.max(d, axis=1, keepdims=True))
        qk = jnp.dot(q, kt, preferred_element_type=F32) * scale
        w = jnp.exp(d - mt) * qk
        s_prev = jnp.exp(inter - mt)
        num = jnp.dot(w.astype(BF16), v, preferred_element_type=F32)
        num = num + s_prev * (jnp.dot(q, ct_prev.astype(BF16), preferred_element_type=F32) * scale)
        qn = jnp.sum(q.astype(F32) * n_prev, axis=1, keepdims=True) * scale
        den = jnp.sum(w, axis=1, keepdims=True) + s_prev * qn
        hh = num / jnp.maximum(jnp.abs(den), jnp.exp(-mt))

        bl = bc[L - 1:L, :]
        a_col = bl - bc + ic
        a_row = bl - br + ir
        m_new = jnp.maximum(bl + m_prev, jnp.max(a_row, axis=1, keepdims=True))
        decay = jnp.exp(bl + m_prev - m_new)
        wa = jnp.exp(a_col - m_new)
        vf = v.astype(F32)
        ct_ref[h] = decay * ct_prev + jnp.dot(kt, (wa * vf).astype(BF16), preferred_element_type=F32)
        n_ref[h:h + 1, :] = decay * n_prev + jnp.sum(wa * k.astype(F32), axis=0, keepdims=True)
        m_ref[h:h + 1, :] = jnp.broadcast_to(m_new, (1, LANES))

        gain = g_ref[:, h * ML_V:(h + 1) * ML_V]
        og = o_ref[:, h * ML_V:(h + 1) * ML_V].astype(F32)
        out_ref[:, h * ML_V:(h + 1) * ML_V] = (_rms(hh, gain) * jax.nn.sigmoid(og)).astype(out_ref.dtype)


def mlstm(proj, kt, if_col, if_row, gain, *, L):
    s = proj.shape[0]
    qk_w = ML_HEADS * ML_QK
    return pl.pallas_call(
        functools.partial(_mlstm_kernel, L=L),
        grid=(s // L,),
        in_specs=[
            pl.BlockSpec((L, qk_w), lambda c: (c, 0)),
            pl.BlockSpec((qk_w, L), lambda c: (0, c)),
            pl.BlockSpec((L, qk_w), lambda c: (c, 1)),
            pl.BlockSpec((L, ML_WIDTH), lambda c: (c, 1)),
            pl.BlockSpec((L, ML_WIDTH), lambda c: (c, 2)),
            pl.BlockSpec((L, LANES), lambda c: (c, 0)),
            pl.BlockSpec((2 * ML_HEADS, L), lambda c: (0, c)),
            pl.BlockSpec((1, ML_WIDTH), lambda c: (0, 0)),
        ],
        out_specs=pl.BlockSpec((L, ML_WIDTH), lambda c: (c, 0)),
        out_shape=jax.ShapeDtypeStruct((s, ML_WIDTH), BF16),
        scratch_shapes=[
            pltpu.VMEM((ML_HEADS, ML_QK, ML_V), F32),
            pltpu.VMEM((ML_HEADS, ML_QK), F32),
            pltpu.VMEM((ML_HEADS, LANES), F32),
        ],
        compiler_params=_cparams("arbitrary"),
        name="mlstm",
    )(proj, kt, proj, proj, proj, if_col, if_row, gain.reshape(1, ML_WIDTH))


def _conv_kernel(b_ref, c_ref, h_ref, w_ref, o_ref, carry_ref):
    @pl.when(pl.program_id(0) == 0)
    def _():
        carry_ref[...] = jnp.zeros_like(carry_ref)

    u = c_ref[...].astype(F32) * h_ref[...].astype(F32)
    rows = u.shape[0]
    ext = jnp.concatenate([carry_ref[...], u], axis=0)
    u1 = ext[7:7 + rows]
    u2 = ext[6:6 + rows]
    w = w_ref[...]
    y = w[0:1] * u2 + w[1:2] * u1 + w[2:3] * u
    o_ref[...] = (b_ref[...].astype(F32) * y).astype(o_ref.dtype)
    carry_ref[...] = u[rows - 8:rows]


def short_conv(proj, w, *, tr):
    s = proj.shape[0]
    base = 0
    return pl.pallas_call(
        _conv_kernel,
        grid=(s // tr,),
        in_specs=[
            pl.BlockSpec((tr, CV_WIDTH), lambda i: (i, base)),
            pl.BlockSpec((tr, CV_WIDTH), lambda i: (i, base + 1)),
            pl.BlockSpec((tr, CV_WIDTH), lambda i: (i, base + 2)),
            pl.BlockSpec((CV_K, CV_WIDTH), lambda i: (0, 0)),
        ],
        out_specs=pl.BlockSpec((tr, CV_WIDTH), lambda i: (i, 0)),
        out_shape=jax.ShapeDtypeStruct((s, CV_WIDTH), BF16),
        scratch_shapes=[pltpu.VMEM((8, CV_WIDTH), F32)],
        compiler_params=_cparams("arbitrary"),
        name="short_conv",
    )(proj, proj, proj, w)


def _da_prep_kernel(q_ref, k_ref, qg_ref, kg_ref, qo_ref, ko_ref):
    r = lax.broadcasted_iota(I32, (LANES, LANES), 0) // DA_QK
    c = lax.broadcasted_iota(I32, (LANES, LANES), 1) // DA_QK
    seg = (r == c).astype(F32)

    def norm(x_ref, g_ref, o_ref, scale):
        for j in range(x_ref.shape[1] // LANES):
            x = x_ref[:, j * LANES:(j + 1) * LANES].astype(F32)
            ss = jnp.dot(x * x, seg, precision=lax.Precision.HIGHEST, preferred_element_type=F32)
            y = x * lax.rsqrt(ss * (1.0 / DA_QK) + EPS) * g_ref[:, j * LANES:(j + 1) * LANES]
            o_ref[:, j * LANES:(j + 1) * LANES] = (y * scale).astype(o_ref.dtype)

    norm(q_ref, qg_ref, qo_ref, DA_QK ** -0.5)
    norm(k_ref, kg_ref, ko_ref, 1.0)


def da_prep(proj, q_g, k_g, *, tr):
    s = proj.shape[0]
    w = DA_HEADS * 2 * DA_QK
    base = 3 * CV_WIDTH // w
    qg = jnp.tile(q_g, w // DA_QK).reshape(1, w)
    kg = jnp.tile(k_g, w // DA_QK).reshape(1, w)
    return pl.pallas_call(
        _da_prep_kernel,
        grid=(s // tr,),
        in_specs=[
            pl.BlockSpec((tr, w), lambda i: (i, base)),
            pl.BlockSpec((tr, w), lambda i: (i, base + 1)),
            pl.BlockSpec((1, w), lambda i: (0, 0)),
            pl.BlockSpec((1, w), lambda i: (0, 0)),
        ],
        out_specs=[pl.BlockSpec((tr, w), lambda i: (i, 0)), pl.BlockSpec((tr, w), lambda i: (i, 0))],
        out_shape=[jax.ShapeDtypeStruct((s, w), BF16), jax.ShapeDtypeStruct((s, w), BF16)],
        compiler_params=_cparams("parallel"),
        name="da_prep",
    )(proj, proj, qg, kg)


def _da_kernel(q_ref, k_ref, v_ref, lam_ref, g_ref, o_ref, *, tq, lam_init):
    qi = pl.program_id(1)
    q = q_ref[...]
    lane = lax.broadcasted_iota(I32, q.shape, 1)
    zero = jnp.zeros_like(q)
    q1 = jnp.where(lane < DA_QK, q, zero)
    q2 = jnp.where(lane >= DA_QK, q, zero)

    def scores(kb):
        return (lax.dot_general(q1, kb, _NT, preferred_element_type=F32),
                lax.dot_general(q2, kb, _NT, preferred_element_type=F32))

    def update(carry, s, vb):
        m, l, acc = carry
        m_new = jnp.maximum(m, jnp.max(s, axis=1, keepdims=True))
        alpha = jnp.exp(m - m_new)
        p = jnp.exp(s - m_new)
        l = alpha * l + jnp.sum(p, axis=1, keepdims=True)
        acc = alpha * acc + jnp.dot(p.astype(BF16), vb, preferred_element_type=F32)
        return m_new, l, acc

    def body(ki, carry):
        c1, c2 = carry
        off = pl.multiple_of(ki * tq, tq)
        kb = k_ref[pl.ds(off, tq), :]
        vb = v_ref[pl.ds(off, tq), :]
        s1, s2 = scores(kb)
        return update(c1, s1, vb), update(c2, s2, vb)

    def init():
        return (jnp.full((tq, 1), -jnp.inf, F32), jnp.zeros((tq, 1), F32), jnp.zeros((tq, DA_V), F32))

    c1, c2 = lax.fori_loop(0, qi, body, (init(), init()))
    off = pl.multiple_of(qi * tq, tq)
    kb = k_ref[pl.ds(off, tq), :]
    vb = v_ref[pl.ds(off, tq), :]
    s1, s2 = scores(kb)
    rq = lax.broadcasted_iota(I32, (tq, tq), 0) // CHUNK
    ck = lax.broadcasted_iota(I32, (tq, tq), 1) // CHUNK
    vis = ck <= rq
    _, l1, a1 = update(c1, jnp.where(vis, s1, -jnp.inf), vb)
    _, l2, a2 = update(c2, jnp.where(vis, s2, -jnp.inf), vb)

    lp = lam_ref[...]
    lam = (jnp.exp(jnp.sum(lp[0:1] * lp[1:2], axis=1, keepdims=True))
           - jnp.exp(jnp.sum(lp[2:3] * lp[3:4], axis=1, keepdims=True)) + lam_init)
    o = a1 / l1 - lam * (a2 / l2)
    o_ref[...] = (_rms(o, g_ref[...]) * (1.0 - lam_init)).astype(o_ref.dtype)


def diff_attention(qn, kn, proj, lam_p, subln_g, *, tq, lam_init):
    s = qn.shape[0]
    vbase = (3 * CV_WIDTH + 2 * DA_HEADS * 2 * DA_QK) // DA_V
    return pl.pallas_call(
        functools.partial(_da_kernel, tq=tq, lam_init=lam_init),
        grid=(DA_HEADS, s // tq),
        in_specs=[
            pl.BlockSpec((tq, 2 * DA_QK), lambda h, i: (i, h)),
            pl.BlockSpec((s, 2 * DA_QK), lambda h, i: (0, h)),
            pl.BlockSpec((s, DA_V), lambda h, i: (0, vbase + h)),
            pl.BlockSpec((4, DA_QK), lambda h, i: (0, 0)),
            pl.BlockSpec((1, DA_V), lambda h, i: (0, 0)),
        ],
        out_specs=pl.BlockSpec((tq, DA_V), lambda h, i: (i, h)),
        out_shape=jax.ShapeDtypeStruct((s, DA_WIDTH), BF16),
        compiler_params=_cparams("parallel", "arbitrary"),
        name="diff_attention",
    )(qn, kn, proj, lam_p, subln_g.reshape(1, DA_V))


def _merge_kernel(h_ref, hm_ref, hc_ref, hd_ref, wg0, wg1, wg2, bg0, bg1, bg2, wbm, wbc, wbd, o_ref):
    h = h_ref[...]

    def gated(wg, bg, hb_ref, wb):
        gate = jax.nn.sigmoid(jnp.dot(h, wg[...], preferred_element_type=F32) + bg[...])
        return gate * jnp.dot(hb_ref[...], wb[...], preferred_element_type=F32)

    out = gated(wg0, bg0, hm_ref, wbm) + gated(wg1, bg1, hc_ref, wbc) + gated(wg2, bg2, hd_ref, wbd)
    o_ref[...] = out.astype(o_ref.dtype)


def gated_merge(h, hm, hc, hd, w_gate, b_gate, w_branch, *, tm, tn):
    s, d = h.shape
    nj = d // tn
    r_c = ML_WIDTH // CV_WIDTH
    r_d = (ML_WIDTH + CV_WIDTH) // DA_WIDTH

    def gspec(b):
        return pl.BlockSpec((d, tn), lambda i, j: (0, j + b * nj))

    def bspec(b):
        return pl.BlockSpec((1, tn), lambda i, j: (0, j + b * nj))

    return pl.pallas_call(
        _merge_kernel,
        grid=(s // tm, nj),
        in_specs=[
            pl.BlockSpec((tm, d), lambda i, j: (i, 0)),
            pl.BlockSpec((tm, ML_WIDTH), lambda i, j: (i, 0)),
            pl.BlockSpec((tm, CV_WIDTH), lambda i, j: (i, 0)),
            pl.BlockSpec((tm, DA_WIDTH), lambda i, j: (i, 0)),
            gspec(0), gspec(1), gspec(2), bspec(0), bspec(1), bspec(2),
            pl.BlockSpec((ML_WIDTH, tn), lambda i, j: (0, j)),
            pl.BlockSpec((CV_WIDTH, tn), lambda i, j: (r_c, j)),
            pl.BlockSpec((DA_WIDTH, tn), lambda i, j: (r_d, j)),
        ],
        out_specs=pl.BlockSpec((tm, tn), lambda i, j: (i, j)),
        out_shape=jax.ShapeDtypeStruct((s, d), BF16),
        compiler_params=_cparams("parallel", "arbitrary"),
        name="gated_merge",
    )(h, hm, hc, hd, w_gate, w_gate, w_gate, b_gate, b_gate, b_gate, w_branch, w_branch, w_branch)


def _xa_kv_kernel(mem_ref, mg_ref, wkv_ref, kg_ref, k_ref, v_ref):
    mn = _rms(mem_ref[...], mg_ref[...]).astype(BF16)
    kv = jnp.dot(mn, wkv_ref[...], preferred_element_type=F32)
    for hh in range(XA_HEADS):
        kh = kv[:, hh * XA_DIM:(hh + 1) * XA_DIM]
        k_ref[:, hh * XA_DIM:(hh + 1) * XA_DIM] = _rms(kh, kg_ref[...]).astype(k_ref.dtype)
    v_ref[...] = kv[:, XA_WIDTH:].astype(v_ref.dtype)


def xa_kv(mem, mem_g, wkv, k_g):
    m, d = mem.shape
    return pl.pallas_call(
        _xa_kv_kernel,
        out_shape=[jax.ShapeDtypeStruct((m, XA_WIDTH), BF16), jax.ShapeDtypeStruct((m, XA_WIDTH), BF16)],
        compiler_params=pltpu.CompilerParams(vmem_limit_bytes=VMEM_LIMIT),
        name="xa_kv",
    )(mem, mem_g.reshape(1, d), wkv, k_g.reshape(1, XA_DIM))


def _xa_kernel(x_ref, ng_ref, wq_ref, qg_ref, k_ref, v_ref, wo_ref, fg_ref, y_ref, yn_ref):
    x = x_ref[...]
    xn = _rms(x, ng_ref[...]).astype(BF16)
    q = jnp.dot(xn, wq_ref[...], preferred_element_type=F32)
    outs = []
    for hh in range(XA_HEADS):
        sl = slice(hh * XA_DIM, (hh + 1) * XA_DIM)
        qh = _rms(q[:, sl], qg_ref[...]).astype(BF16)
        s = lax.dot_general(qh, k_ref[:, sl], _NT, preferred_element_type=F32) * (XA_DIM ** -0.5)
        p = jnp.exp(s - jnp.max(s, axis=1, keepdims=True))
        p = p / jnp.sum(p, axis=1, keepdims=True)
        outs.append(jnp.dot(p.astype(BF16), v_ref[:, sl], preferred_element_type=F32))
    o = jnp.concatenate(outs, axis=1).astype(BF16)
    y = x + jnp.dot(o, wo_ref[...], preferred_element_type=F32)
    y_ref[...] = y
    yn_ref[...] = _rms(y, fg_ref[...]).astype(yn_ref.dtype)


def cross_attention(x, norm_g, wq, q_g, k, v, wo, next_g, *, tm):
    s, d = x.shape
    m = k.shape[0]
    const = lambda i: (0, 0)
    return pl.pallas_call(
        _xa_kernel,
        grid=(s // tm,),
        in_specs=[
            pl.BlockSpec((tm, d), lambda i: (i, 0)),
            pl.BlockSpec((1, d), const),
            pl.BlockSpec((d, XA_WIDTH), const),
            pl.BlockSpec((1, XA_DIM), const),
            pl.BlockSpec((m, XA_WIDTH), const),
            pl.BlockSpec((m, XA_WIDTH), const),
            pl.BlockSpec((XA_WIDTH, d), const),
            pl.BlockSpec((1, d), const),
        ],
        out_specs=[pl.BlockSpec((tm, d), lambda i: (i, 0)), pl.BlockSpec((tm, d), lambda i: (i, 0))],
        out_shape=[jax.ShapeDtypeStruct((s, d), F32), jax.ShapeDtypeStruct((s, d), BF16)],
        compiler_params=_cparams("parallel"),
        name="cross_attention",
    )(x, norm_g.reshape(1, d), wq, q_g.reshape(1, XA_DIM), k, v, wo, next_g.reshape(1, d))


def _top16(s, payload=None):
    n = s.shape[0]
    iota = lax.broadcasted_iota(I32, s.shape, 0)
    vals, sel = [], []
    for _ in range(PEER_TOPK):
        mx = jnp.max(s, axis=0, keepdims=True)
        ix = jnp.min(jnp.where(s == mx, iota, n), axis=0, keepdims=True)
        hit = iota == ix
        vals.append(mx)
        if payload is None:
            sel.append(ix)
        else:
            sel.append(jnp.max(jnp.where(hit, payload, -1), axis=0, keepdims=True))
        s = jnp.where(hit, -jnp.inf, s)
    return jnp.concatenate(vals, axis=0), jnp.concatenate(sel, axis=0)


def _peer_topk_kernel(q_ref, keys_ref, idx_ref, g_ref):
    q = q_ref[...].astype(BF16)
    k1 = keys_ref[0, 0].astype(BF16)
    k2 = keys_ref[0, 1].astype(BF16)
    s1 = lax.dot_general(k1, q[:, :PEER_HALF], _NT, preferred_element_type=F32)
    s2 = lax.dot_general(k2, q[:, PEER_HALF:], _NT, preferred_element_type=F32)
    t1, i1 = _top16(s1)
    t2, i2 = _top16(s2)
    cand = jnp.concatenate([t1[a:a + 1] + t2 for a in range(PEER_TOPK)], axis=0)
    cidx = jnp.concatenate([i1[a:a + 1] * N_KEYS + i2 for a in range(PEER_TOPK)], axis=0)
    top, eidx = _top16(cand, cidx)
    e = jnp.exp(top - top[0:1])
    idx_ref[0] = eidx * PEER_ROWS
    g_ref[0] = e / jnp.sum(e, axis=0, keepdims=True)


def peer_topk(q, keys, *, tt):
    s = q.shape[0]
    return pl.pallas_call(
        _peer_topk_kernel,
        grid=(s // tt, PEER_HEADS),
        in_specs=[
            pl.BlockSpec((tt, 2 * PEER_HALF), lambda i, h: (i, h)),
            pl.BlockSpec((1, 2, N_KEYS, PEER_HALF), lambda i, h: (h, 0, 0, 0)),
        ],
        out_specs=[pl.BlockSpec((1, PEER_TOPK, tt), lambda i, h: (h, 0, i)),
                   pl.BlockSpec((1, PEER_TOPK, tt), lambda i, h: (h, 0, i))],
        out_shape=[jax.ShapeDtypeStruct((PEER_HEADS, PEER_TOPK, s), I32),
                   jax.ShapeDtypeStruct((PEER_HEADS, PEER_TOPK, s), F32)],
        compiler_params=_cparams("parallel", "parallel"),
        name="peer_topk",
    )(q, keys)


PEER_NBUF = 8
PEER_AHEAD = 5
PEER_GROUP = 8
PEER_UROWS = 16
PEER_ROWS = 2 * PEER_UROWS
PEER_PITCH = 40
HI_MASK = 0xFFFF0000


def _peer_mix_kernel(idx_ref, idx_next_ref, gt_ref, x_ref, ng_ref, tab_ref, o_ref,
                     idx_smem, xn_ref, xb_ref, ps_ref, g_ref, buf, sems, idx_sem, *, tb, half):
    step = pl.program_id(0)
    cp0 = pltpu.make_async_copy(idx_ref, idx_smem.at[:, pl.ds(0, tb)], idx_sem.at[0])
    cp1 = pltpu.make_async_copy(idx_next_ref, idx_smem.at[:, pl.ds(tb, LANES)], idx_sem.at[1])
    cp0.start()
    cp1.start()
    xn_ref[...] = _rms(x_ref[...], ng_ref[...])
    g_ref[...] = gt_ref[...].T
    cp0.wait()
    cp1.wait()

    def issue_rows(t, slot, e0, e1):
        for e in range(e0, e1):
            src = tab_ref.at[pl.ds(pl.multiple_of(idx_smem[e, t], PEER_ROWS), PEER_ROWS), :]
            dst = buf.at[slot, pl.ds(e * PEER_PITCH, PEER_ROWS), :]
            pltpu.make_async_copy(src, dst, sems.at[slot]).start(priority=e % 2)

    def wait(slot):
        nrow = PEER_SEL * PEER_ROWS
        pltpu.make_async_copy(tab_ref.at[pl.ds(0, nrow), :], buf.at[slot, pl.ds(0, nrow), :], sems.at[slot]).wait()

    ones8 = jnp.ones((8, LANES), BF16)
    er = lax.broadcasted_iota(I32, (PEER_SEL, 2 * PEER_SEL), 0)
    ec = lax.broadcasted_iota(I32, (PEER_SEL, 2 * PEER_SEL), 1)
    expand = (ec // 2 == er).astype(BF16)
    lane = lax.broadcasted_iota(I32, (8, 2 * PEER_SEL), 1)
    sub = lax.broadcasted_iota(I32, (8, 2 * PEER_SEL), 0)
    pick = sub == (lane % 2)
    ngrp = PEER_SEL // PEER_GROUP

    def rows(slot, grp, r):
        return buf[slot, pl.ds(grp * PEER_GROUP * PEER_PITCH + r, PEER_GROUP, stride=PEER_PITCH), :]

    def key_group(slot, grp, r0, r1):
        acc = None
        for r in range(r0, r1):
            w = rows(slot, grp, r)
            lo = pltpu.bitcast(w << 16, F32)
            hi = pltpu.bitcast(w & jnp.uint32(HI_MASK), F32)
            term = lo * xb_ref[:, r * LANES:(r + 1) * LANES] + hi * xb_ref[:, half + r * LANES:half + (r + 1) * LANES]
            acc = term if acc is None else acc + term
        return acc

    def key_scores(par):
        psum = ps_ref[par]
        p_hi = psum.astype(BF16)
        p_lo = (psum - p_hi.astype(F32)).astype(BF16)
        return (lax.dot_general(ones8, p_hi, _NT, preferred_element_type=F32)
                + lax.dot_general(ones8, p_lo, _NT, preferred_element_type=F32))

    def gate_weights(t, act):
        wgt = jax.nn.gelu(act) * g_ref[pl.ds(t, 1), :]
        wexp = jnp.dot(wgt.astype(BF16), expand, preferred_element_type=F32)
        return jnp.where(pick, wexp, 0.0).astype(BF16)

    def value_dot(slot, r, w8):
        vw = jnp.concatenate([rows(slot, grp, PEER_UROWS + r) for grp in range(ngrp)], axis=0)
        o = jnp.dot(w8, pltpu.bitcast(vw, BF16), preferred_element_type=F32)
        return o[0:1], o[1:2]

    def finish(t, lo, hi):
        o_ref[pl.ds(t, 1), :] = x_ref[pl.ds(t, 1), :] + jnp.concatenate(lo + hi, axis=1)

    def region(t, k, w8_prev):
        wait(k)
        xb_ref[...] = jnp.broadcast_to(xn_ref[pl.ds(t, 1), :], xb_ref.shape)
        t_gate = jnp.maximum(t - 1, 0)
        t_mix = jnp.maximum(t - 2, 0)
        parts, lo, hi = [], [], []
        for part in range(2):
            if part == 0:
                act = key_scores((k - 1) % 2)
            else:
                w8 = gate_weights(t_gate, act)
                w8 = jnp.where(t >= 1, w8, jnp.zeros_like(w8))
            for r in range(part * PEER_UROWS // 2, (part + 1) * PEER_UROWS // 2):
                a, b = value_dot((k - 2) % PEER_NBUF, r, w8_prev)
                lo.append(a)
                hi.append(b)
            accs = [None] * (ngrp // 2)
            for r0 in range(0, PEER_UROWS, PEER_UROWS // 2):
                for gi in range(ngrp // 2):
                    a = key_group(k, part * ngrp // 2 + gi, r0, r0 + PEER_UROWS // 2)
                    accs[gi] = a if accs[gi] is None else accs[gi] + a
            parts.extend(accs)
            issue_rows(t + PEER_AHEAD, (k + PEER_AHEAD) % PEER_NBUF, part * PEER_SEL // 2, (part + 1) * PEER_SEL // 2)
        ps_ref[k % 2] = jnp.concatenate(parts, axis=0)
        finish(t_mix, lo, hi)
        return w8

    @pl.when(step == 0)
    def _():
        ps_ref[...] = jnp.zeros_like(ps_ref)
        for k in range(PEER_NBUF - 2, PEER_NBUF):
            buf[k] = jnp.zeros(buf.shape[1:], U32)

        def prime(k, _):
            issue_rows(k, k, 0, PEER_SEL)
            return 0

        lax.fori_loop(0, PEER_AHEAD, prime, 0)

    def body(i, w8):
        for k in range(PEER_NBUF):
            w8 = region(i * PEER_NBUF + k, k, w8)
        return w8

    w8 = lax.fori_loop(0, tb // PEER_NBUF, body, jnp.zeros((8, 2 * PEER_SEL), BF16))
    lo, hi = zip(*[value_dot(PEER_NBUF - 2, r, w8) for r in range(PEER_UROWS)])
    finish(tb - 2, list(lo), list(hi))
    w8 = gate_weights(tb - 1, key_scores((PEER_NBUF - 1) % 2))
    lo, hi = zip(*[value_dot(PEER_NBUF - 1, r, w8) for r in range(PEER_UROWS)])
    finish(tb - 1, list(lo), list(hi))

    @pl.when(step == pl.num_programs(0) - 1)
    def _():
        for k in range(PEER_AHEAD):
            wait(k)


def peer_mix(eidx, gt, x, norm_g, table, *, tb):
    s, d = x.shape
    half = d // 2
    assert half == PEER_UROWS * LANES and tb % PEER_NBUF == 0 and tb // PEER_NBUF >= 2 and tb % LANES == 0
    assert PEER_AHEAD <= LANES
    eidx = jnp.pad(eidx, ((0, 0), (0, LANES)))
    nb = tb // LANES
    return pl.pallas_call(
        functools.partial(_peer_mix_kernel, tb=tb, half=half),
        grid=(s // tb,),
        in_specs=[
            pl.BlockSpec((PEER_SEL, tb), lambda i: (0, i)),
            pl.BlockSpec((PEER_SEL, LANES), lambda i: (0, (i + 1) * nb)),
            pl.BlockSpec((PEER_SEL, tb), lambda i: (0, i)),
            pl.BlockSpec((tb, d), lambda i: (i, 0)),
            pl.BlockSpec((1, d), lambda i: (0, 0)),
            pl.BlockSpec(memory_space=pl.ANY),
        ],
        out_specs=pl.BlockSpec((tb, d), lambda i: (i, 0)),
        out_shape=jax.ShapeDtypeStruct((s, d), F32),
        scratch_shapes=[
            pltpu.SMEM((PEER_SEL, tb + LANES), I32),
            pltpu.VMEM((tb, d), F32),
            pltpu.VMEM((8, d), F32),
            pltpu.VMEM((2, PEER_SEL, LANES), F32),
            pltpu.VMEM((tb, PEER_SEL), F32),
            pltpu.VMEM((PEER_NBUF, PEER_SEL * PEER_PITCH, LANES), U32),
            pltpu.SemaphoreType.DMA((PEER_NBUF,)),
            pltpu.SemaphoreType.DMA((2,)),
        ],
        compiler_params=_cparams("arbitrary"),
        name="peer_mix",
    )(eidx, eidx, gt, x, norm_g.reshape(1, d), table)


def _pack_kernel(u_ref, v_ref, o_ref):
    ne, d = u_ref.shape
    half = d // 2

    def words(w_ref):
        lo = pltpu.bitcast(w_ref[:, :half].astype(BF16).astype(F32), U32)
        hi = pltpu.bitcast(w_ref[:, half:].astype(BF16).astype(F32), U32)
        return (lo >> 16) | hi

    for tab, base in ((words(u_ref), 0), (words(v_ref), PEER_UROWS)):
        for r in range(PEER_UROWS):
            o_ref[pl.ds(base + r, ne, stride=PEER_ROWS), :] = tab[:, r * LANES:(r + 1) * LANES]


def _pack_expert_table(u, v, layer, *, te=128):
    _, n, d = u.shape
    assert d == 2 * PEER_UROWS * LANES
    return pl.pallas_call(
        _pack_kernel,
        grid=(n // te,),
        in_specs=[pl.BlockSpec((None, te, d), lambda i: (layer, i, 0)),
                  pl.BlockSpec((None, te, d), lambda i: (layer, i, 0))],
        out_specs=pl.BlockSpec((te * PEER_ROWS, LANES), lambda i: (i, 0)),
        out_shape=jax.ShapeDtypeStruct((n * PEER_ROWS, LANES), U32),
        compiler_params=_cparams("parallel"),
        name="peer_pack",
    )(u, v)


def _layer(x, h, mem, lam_init, p):
    s, d = x.shape
    n_qkvo = 2 * ML_HEADS * ML_QK + 2 * ML_WIDTH
    proj_ml = matmul(h, p["w_ml"], tm=1024, tn=1024, out_dtype=BF16, name="in_proj_mlstm")
    proj_cd = matmul(h, p["w_cd"], tm=1024, tn=1024, out_dtype=BF16, name="in_proj_conv_da")
    gates_if = matmul(h, p["w_if"], tm=1024, tn=LANES, out_dtype=F32, bias=p["b_if"], name="if_proj")
    kt = proj_ml[:, ML_HEADS * ML_QK:2 * ML_HEADS * ML_QK].T
    if_row = gates_if[:, :2 * ML_HEADS].T
    hm = mlstm(proj_ml, kt, gates_if, if_row, p["ml_norm"], L=128)
    hc = short_conv(proj_cd, p["conv_w"], tr=512)
    qn, kn = da_prep(proj_cd, p["da_q_norm"], p["da_k_norm"], tr=512)
    hd = diff_attention(qn, kn, proj_cd, p["da_lambda"], p["da_subln"], tq=512, lam_init=lam_init)
    merged = gated_merge(h, hm, hc, hd, p["w_gate"], p["b_gate"], p["w_branch"], tm=512, tn=256)
    x = matmul(merged, p["w_out"], tm=512, tn=1024, out_dtype=F32, residual=x, name="out_proj")
    k, v = xa_kv(mem, p["mem_norm"], p["xa_wkv"], p["xa_k_norm"])
    x, xn = cross_attention(x, p["xa_norm"], p["xa_wq"], p["xa_q_norm"], k, v, p["xa_wo"], p["ffn_norm"], tm=256)
    q = matmul(xn, p["peer_wq"], tm=1024, tn=1024, out_dtype=F32, name="peer_q")
    eidx, g = peer_topk(q, p["peer_keys"], tt=512)
    return peer_mix(eidx.reshape(PEER_SEL, s), g.reshape(PEER_SEL, s), x, p["ffn_norm"], p["table"], tb=256)


def kernel(x, mem, mix_norm, w_in, b_if, ml_norm, conv_w, da_q_norm, da_k_norm, da_lambda, da_subln, w_gate, b_gate, w_branch, w_out, xa_norm, mem_norm, xa_wq, xa_wkv, xa_q_norm, xa_k_norm, xa_wo, ffn_norm, peer_wq, peer_keys, peer_u, peer_v):
    depth = w_in.shape[0]
    d = x.shape[-1]
    xs = x.reshape(-1, d)
    mems = mem.reshape(-1, d)
    n_qkvo = 2 * ML_HEADS * ML_QK + 2 * ML_WIDTH
    for l in range(depth):
        wl = w_in[l]
        w_if = jnp.pad(wl[:, n_qkvo:n_qkvo + 2 * ML_HEADS], ((0, 0), (0, LANES - 2 * ML_HEADS)))
        bias_if = jnp.pad(b_if[l].reshape(1, 2 * ML_HEADS), ((0, 0), (0, LANES - 2 * ML_HEADS)))
        p = dict(
            w_ml=wl[:, :n_qkvo].astype(BF16), w_cd=wl[:, n_qkvo + 2 * ML_HEADS:].astype(BF16),
            w_if=w_if.astype(BF16), b_if=bias_if,
            ml_norm=ml_norm[l], conv_w=conv_w[l], da_q_norm=da_q_norm[l], da_k_norm=da_k_norm[l],
            da_lambda=da_lambda[l], da_subln=da_subln[l],
            w_gate=w_gate[l].astype(BF16), b_gate=b_gate[l].reshape(1, -1), w_branch=w_branch[l].astype(BF16),
            w_out=w_out[l].astype(BF16), xa_norm=xa_norm[l], mem_norm=mem_norm[l], xa_wq=xa_wq[l].astype(BF16),
            xa_wkv=xa_wkv[l].astype(BF16), xa_q_norm=xa_q_norm[l], xa_k_norm=xa_k_norm[l],
            xa_wo=xa_wo[l].astype(BF16), ffn_norm=ffn_norm[l], peer_wq=peer_wq[l].astype(BF16),
            peer_keys=peer_keys[l], table=_pack_expert_table(peer_u, peer_v, l),
        )
        h = rmsnorm(xs, mix_norm[l], tr=256)
        lam_init = 0.8 - 0.6 * math.exp(-0.3 * l)
        xs = _layer(xs, h, mems, lam_init, p)
    return xs.reshape(x.shape)
```
